```python
import math
import jax, jax.numpy as jnp
from jax import lax
import numpy as np

D_MODEL = 2048
BATCH = 2
SEQ = 4096
DEPTH = 4
DEC_BATCH = 8
DEC_SEQ = 1
PAST_LEN = 16384
PAGE_SIZE = 128

HEAD_DIM = 128
N_HEADS = D_MODEL // HEAD_DIM
H_A = N_HEADS // 2
H_B = N_HEADS - H_A
H_AB = H_A + H_B
H_C = N_HEADS
H_IDX = 4
D_IDX = 64
DSA_TOPK = 256
MOBA_BLOCK = 256
MOBA_TOPK = 3
DSA_QBLOCK = 128
MOBA_QBLOCK = 32
FOX_QBLOCK = 128
NUM_BUCKETS = 32
MAX_DISTANCE = 128
D_FF = 5632
CONV_W = 3
FORGET_BIAS = 3.0
RMS_EPS = 1e-6
NEG_INF = -1e30
N_AB_LAYERS = (DEPTH + 1) // 2
N_C_LAYERS = DEPTH // 2
D_IN_AB = 3 * H_A * HEAD_DIM + 3 * H_B * HEAD_DIM + H_IDX * D_IDX + D_IDX + H_IDX
D_IN_C = 3 * H_C * HEAD_DIM + H_C

kernel_name = 'hybrid_dsa_moba_fox_convffn_step'


def rmsnorm(x, g):
    xf = x.astype(jnp.float32)
    y = xf * lax.rsqrt(jnp.mean(xf * xf, axis=-1, keepdims=True) + RMS_EPS)
    return (y * g.astype(jnp.float32)).astype(x.dtype)


def t5_bucket(dist):
    dist = jnp.maximum(dist, 0)
    max_exact = NUM_BUCKETS // 2
    log_ratio = jnp.log(jnp.maximum(dist, 1).astype(jnp.float32) / max_exact) / math.log(MAX_DISTANCE / max_exact)
    large = jnp.minimum(max_exact + (log_ratio * (NUM_BUCKETS - max_exact)).astype(jnp.int32), NUM_BUCKETS - 1)
    return jnp.where(dist < max_exact, dist, large)


def qblock(n_q, pref):
    return pref if n_q % pref == 0 else n_q


def sweep_blocks(fn, qb, *xs):
    n_q = xs[0].shape[1]
    nb = n_q // qb

    def split(a):
        return jnp.moveaxis(a.reshape((a.shape[0], nb, qb) + a.shape[2:]), 1, 0)

    out = lax.map(lambda args: fn(*args), tuple(split(a) for a in xs))
    out = jnp.moveaxis(out, 0, 1)
    return out.reshape((out.shape[0], n_q) + out.shape[3:])


def split_cols(h, sizes):
    return jnp.split(h, [int(s) for s in np.cumsum(sizes)[:-1]], axis=-1)


def pad_blocks(a):
    pad = (-a.shape[1]) % MOBA_BLOCK
    return jnp.pad(a, ((0, 0), (0, pad)) + ((0, 0),) * (a.ndim - 2))


def merge_heads(*outs):
    o = jnp.concatenate(outs, axis=2)
    return o.reshape(o.shape[0], o.shape[1], -1)


def ab_project(xn, w_in):
    b, t = xn.shape[0], xn.shape[1]
    h = xn @ w_in
    sizes = (H_A * HEAD_DIM,) * 3 + (H_B * HEAD_DIM,) * 3 + (H_IDX * D_IDX, D_IDX, H_IDX)
    qa, ka, va, qb, kb, vb, qi, ki, wi = split_cols(h, sizes)
    ha = lambda a: a.reshape(b, t, H_A, HEAD_DIM)
    hb = lambda a: a.reshape(b, t, H_B, HEAD_DIM)
    return ha(qa), ha(ka), ha(va), hb(qb), hb(kb), hb(vb), qi.reshape(b, t, H_IDX, D_IDX), ki, wi


def c_project(xn, w_in, b_f):
    b, t = xn.shape[0], xn.shape[1]
    h = xn @ w_in
    q, k, v, fg = split_cols(h, (H_C * HEAD_DIM,) * 3 + (H_C,))
    logf = jax.nn.log_sigmoid(fg.astype(jnp.float32) + b_f.astype(jnp.float32))
    hc = lambda a: a.reshape(b, t, H_C, HEAD_DIM)
    return hc(q), hc(k), hc(v), logf


def fetch_dense(k, v):
    def fetch(idx):
        b_ix = jnp.arange(idx.shape[0])[:, None, None]
        return k[b_ix, idx], v[b_ix, idx]
    return fetch


def fetch_paged(cache_k, cache_v, li, page_table, k_new, v_new):
    def fetch(idx):
        b_ix = jnp.arange(idx.shape[0])[:, None, None]
        past_idx = jnp.minimum(idx, PAST_LEN - 1)
        phys = page_table[b_ix, past_idx // PAGE_SIZE]
        off = past_idx % PAGE_SIZE
        new_idx = jnp.clip(idx - PAST_LEN, 0, k_new.shape[1] - 1)
        is_new = (idx >= PAST_LEN)[..., None, None]
        k = jnp.where(is_new, k_new[b_ix, new_idx], cache_k[li, phys, off, :H_A])
        v = jnp.where(is_new, v_new[b_ix, new_idx], cache_v[li, phys, off, :H_A])
        return k, v
    return fetch


def paged_rows(cache, li, page_table, h0, n_h):
    rows = cache[li, page_table, :, h0:h0 + n_h]
    return rows.reshape(page_table.shape[0], PAST_LEN, n_h, HEAD_DIM)


def dsa_attend(q, q_idx, w_idx, k_idx_all, fetch, q_pos, n_keys, table_a):
    k_sel = min(DSA_TOPK, n_keys // 4)
    key_pos = jnp.arange(n_keys, dtype=jnp.int32)
    scale = HEAD_DIM ** -0.5

    def block(q_blk, qi_blk, wi_blk, posb):
        s_idx = jnp.einsum('bqhd,bsd->bqhs', qi_blk, k_idx_all, preferred_element_type=jnp.float32)
        score = jnp.einsum('bqhs,bqh->bqs', jax.nn.relu(s_idx), wi_blk.astype(jnp.float32))
        score = jnp.where(key_pos[None, None, :] <= posb[:, :, None], score, NEG_INF)
        _, idx = lax.top_k(score, k_sel)
        k_g, v_g = fetch(idx)
        logits = jnp.einsum('bqhd,bqkhd->bqhk', q_blk, k_g, preferred_element_type=jnp.float32) * scale
        dist = posb[:, :, None] - idx
        bias = jnp.moveaxis(table_a[t5_bucket(dist)], -1, 2).astype(jnp.float32)
        logits = jnp.where((dist >= 0)[:, :, None, :], logits + bias, NEG_INF)
        p = jax.nn.softmax(logits, axis=-1).astype(v_g.dtype)
        return jnp.einsum('bqhk,bqkhd->bqhd', p, v_g)

    return sweep_blocks(block, qblock(q.shape[1], DSA_QBLOCK), q, q_idx, w_idx, q_pos)


def moba_attend(q, k_all, v_all, q_pos, table_b):
    bsz, l_pad = k_all.shape[0], k_all.shape[1]
    n_blk = l_pad // MOBA_BLOCK
    n_top = min(MOBA_TOPK, n_blk)
    k_blk = jnp.moveaxis(k_all.reshape(bsz, n_blk, MOBA_BLOCK, H_B, HEAD_DIM), 3, 1)
    v_blk = jnp.moveaxis(v_all.reshape(bsz, n_blk, MOBA_BLOCK, H_B, HEAD_DIM), 3, 1)
    k_mean = jnp.mean(k_blk.astype(jnp.float32), axis=3)
    blk_ids = jnp.arange(n_blk, dtype=jnp.int32)
    in_blk = jnp.arange(MOBA_BLOCK, dtype=jnp.int32)
    b_ix = jnp.arange(bsz)[:, None, None, None]
    h_ix = jnp.arange(H_B)[None, None, :, None]
    h_ix5 = jnp.arange(H_B)[None, None, :, None, None]
    table_t = table_b.T
    scale = HEAD_DIM ** -0.5

    def block(q_blk, posb):
        qb_n = q_blk.shape[1]
        own = posb // MOBA_BLOCK
        gate = jnp.einsum('bqhd,bhnd->bqhn', q_blk.astype(jnp.float32), k_mean)
        gate = jnp.where(blk_ids[None, None, None, :] < own[:, :, None, None], gate, NEG_INF)
        _, sel = lax.top_k(gate, n_top)
        sel_ok = sel < own[:, :, None, None]
        own_b = jnp.broadcast_to(own[:, :, None, None], sel.shape[:3] + (1,))
        blocks = jnp.concatenate([sel, own_b], axis=-1)
        ok = jnp.concatenate([sel_ok, jnp.ones_like(own_b, dtype=bool)], axis=-1)
        k_g = k_blk[b_ix, h_ix, blocks]
        v_g = v_blk[b_ix, h_ix, blocks].reshape(bsz, qb_n, H_B, -1, HEAD_DIM)
        kpos = blocks[..., None] * MOBA_BLOCK + in_blk
        dist = posb[:, :, None, None, None] - kpos
        valid = ok[..., None] & (dist >= 0)
        logits = jnp.einsum('bqhd,bqhnsd->bqhns', q_blk, k_g, preferred_element_type=jnp.float32) * scale
        logits = logits + table_t[h_ix5, t5_bucket(dist)].astype(jnp.float32)
        logits = jnp.where(valid, logits, NEG_INF).reshape(bsz, qb_n, H_B, -1)
        p = jax.nn.softmax(logits, axis=-1).astype(v_g.dtype)
        return jnp.einsum('bqhs,bqhsd->bqhd', p, v_g)

    return sweep_blocks(block, qblock(q.shape[1], MOBA_QBLOCK), q, q_pos)


def fox_attend(q, k_all, v_all, r_all, r_q, q_pos):
    n_keys = k_all.shape[1]
    key_pos = jnp.arange(n_keys, dtype=jnp.int32)
    r_keys = jnp.moveaxis(r_all, 1, 2)
    scale = HEAD_DIM ** -0.5

    def block(q_blk, rq_blk, posb):
        logits = jnp.einsum('bqhd,bshd->bhqs', q_blk, k_all, preferred_element_type=jnp.float32) * scale
        logits = logits + r_keys[:, :, None, :] - jnp.moveaxis(rq_blk, 1, 2)[..., None]
        causal = key_pos[None, None, None, :] <= posb[:, None, :, None]
        p = jax.nn.softmax(jnp.where(causal, logits, NEG_INF), axis=-1).astype(v_all.dtype)
        return jnp.einsum('bhqs,bshd->bqhd', p, v_all)

    return sweep_blocks(block, qblock(q.shape[1], FOX_QBLOCK), q, r_q, q_pos)


def suffix_logf(logf):
    return lax.cumsum(logf, axis=1, reverse=True) - logf


def conv_ffn(xn, w_up, conv_w, conv_b, w_down, prev):
    u = xn @ w_up
    t = u.shape[1]
    ext = jnp.concatenate([prev.astype(u.dtype), u], axis=1)
    c = conv_b + sum(conv_w[i] * ext[:, i:i + t] for i in range(CONV_W))
    gate, val = jnp.split(c, 2, axis=-1)
    return (jax.nn.silu(gate) * val) @ w_down, ext[:, t:]


def setup_inputs(seed: int = 0) -> dict:
    key = jax.random.key(seed)
    ks = jax.random.split(key, 24)
    f32 = jnp.float32
    n_pages = PAST_LEN // PAGE_SIZE
    n_used = DEC_BATCH * n_pages
    n_phys = n_used + (n_used + 3) // 4

    def nrm(k, shape, scale=1.0):
        return jax.random.normal(k, shape, f32) * scale

    x_prompt = nrm(ks[0], (BATCH, SEQ, D_MODEL))
    x_sample = nrm(ks[1], (DEC_BATCH, DEC_SEQ, D_MODEL))
    cache_k_ab = nrm(ks[2], (N_AB_LAYERS, n_phys, PAGE_SIZE, H_AB, HEAD_DIM))
    cache_v_ab = nrm(ks[3], (N_AB_LAYERS, n_phys, PAGE_SIZE, H_AB, HEAD_DIM))
    cache_kidx = nrm(ks[4], (N_AB_LAYERS, n_phys, PAGE_SIZE, D_IDX))
    cache_k_c = nrm(ks[5], (N_C_LAYERS, n_phys, PAGE_SIZE, H_C, HEAD_DIM))
    cache_v_c = nrm(ks[6], (N_C_LAYERS, n_phys, PAGE_SIZE, H_C, HEAD_DIM))
    cache_logf_c = jax.nn.log_sigmoid(FORGET_BIAS + nrm(ks[7], (N_C_LAYERS, n_phys, PAGE_SIZE, H_C)))
    state_conv = nrm(ks[8], (DEPTH, DEC_BATCH, CONV_W - 1, 2 * D_FF))
    page_table = jax.random.permutation(ks[9], n_phys)[:n_used].reshape(DEC_BATCH, n_pages).astype(jnp.int32)
    g_mix = 1.0 + nrm(ks[10], (DEPTH, D_MODEL), 0.05)
    g_ffn = 1.0 + nrm(ks[11], (DEPTH, D_MODEL), 0.05)
    g_final = 1.0 + nrm(ks[12], (D_MODEL,), 0.05)
    w_in_ab = nrm(ks[13], (N_AB_LAYERS, D_MODEL, D_IN_AB), D_MODEL ** -0.5)
    w_out_ab = nrm(ks[14], (N_AB_LAYERS, H_AB * HEAD_DIM, D_MODEL), (H_AB * HEAD_DIM) ** -0.5)
    t5_table = nrm(ks[15], (NUM_BUCKETS, H_AB), 0.5)
    w_in_c = nrm(ks[16], (N_C_LAYERS, D_MODEL, D_IN_C), D_MODEL ** -0.5)
    b_forget = FORGET_BIAS + nrm(ks[17], (N_C_LAYERS, H_C), 0.5)
    w_out_c = nrm(ks[18], (N_C_LAYERS, H_C * HEAD_DIM, D_MODEL), (H_C * HEAD_DIM) ** -0.5)
    w_up = nrm(ks[19], (DEPTH, D_MODEL, 2 * D_FF), D_MODEL ** -0.5)
    conv_w = nrm(ks[20], (DEPTH, CONV_W, 2 * D_FF), CONV_W ** -0.5)
    conv_b = nrm(ks[21], (DEPTH, 2 * D_FF), 0.02)
    w_down = nrm(ks[22], (DEPTH, D_FF, D_MODEL), D_FF ** -0.5)
    return {'x_prompt': x_prompt, 'x_sample': x_sample, 'cache_k_ab': cache_k_ab, 'cache_v_ab': cache_v_ab,
            'cache_kidx': cache_kidx, 'cache_k_c': cache_k_c, 'cache_v_c': cache_v_c,
            'cache_logf_c': cache_logf_c, 'state_conv': state_conv, 'page_table': page_table,
            'g_mix': g_mix, 'g_ffn': g_ffn, 'g_final': g_final, 'w_in_ab': w_in_ab, 'w_out_ab': w_out_ab,
            't5_table': t5_table, 'w_in_c': w_in_c, 'b_forget': b_forget, 'w_out_c': w_out_c,
            'w_up': w_up, 'conv_w': conv_w, 'conv_b': conv_b, 'w_down': w_down}


def reference(x_prompt, x_sample, cache_k_ab, cache_v_ab, cache_kidx, cache_k_c, cache_v_c,
              cache_logf_c, state_conv, page_table, g_mix, g_ffn, g_final, w_in_ab, w_out_ab,
              t5_table, w_in_c, b_forget, w_out_c, w_up, conv_w, conv_b, w_down):
    bsz, seq = x_prompt.shape[0], x_prompt.shape[1]
    dbsz, dseq = x_sample.shape[0], x_sample.shape[1]
    n_keys_s = PAST_LEN + dseq
    pos_p = jnp.arange(seq, dtype=jnp.int32)[None]
    pos_s = PAST_LEN + jnp.arange(dseq, dtype=jnp.int32)[None]
    table_a = t5_table[:, :H_A]
    table_b = t5_table[:, H_A:]
    xp, xs = x_prompt, x_sample
    kab_p, vab_p, kidx_p, kc_p, vc_p, lf_p, cv_p = [], [], [], [], [], [], []
    kab_s, vab_s, kidx_s, kc_s, vc_s, lf_s, cv_s = [], [], [], [], [], [], []
    for layer in range(DEPTH):
        li = layer // 2
        hp = rmsnorm(xp, g_mix[layer])
        hs = rmsnorm(xs, g_mix[layer])
        if layer % 2 == 0:
            qa, ka, va, qb, kb, vb, qi, ki, wi = ab_project(hp, w_in_ab[li])
            oa = dsa_attend(qa, qi, wi, ki, fetch_dense(ka, va), pos_p, seq, table_a)
            ob = moba_attend(qb, pad_blocks(kb), pad_blocks(vb), pos_p, table_b)
            xp = xp + merge_heads(oa, ob) @ w_out_ab[li]
            kab_p.append(jnp.concatenate([ka, kb], axis=2))
            vab_p.append(jnp.concatenate([va, vb], axis=2))
            kidx_p.append(ki)
            qa, ka, va, qb, kb, vb, qi, ki, wi = ab_project(hs, w_in_ab[li])
            kidx_past = cache_kidx[li, page_table].reshape(dbsz, PAST_LEN, D_IDX).astype(ki.dtype)
            kidx_all = jnp.concatenate([kidx_past, ki], axis=1)
            oa = dsa_attend(qa, qi, wi, kidx_all, fetch_paged(cache_k_ab, cache_v_ab, li, page_table, ka, va),
                            pos_s, n_keys_s, table_a)
            kb_all = pad_blocks(jnp.concatenate([paged_rows(cache_k_ab, li, page_table, H_A, H_B).astype(kb.dtype), kb], axis=1))
            vb_all = pad_blocks(jnp.concatenate([paged_rows(cache_v_ab, li, page_table, H_A, H_B).astype(vb.dtype), vb], axis=1))
            ob = moba_attend(qb, kb_all, vb_all, pos_s, table_b)
            xs = xs + merge_heads(oa, ob) @ w_out_ab[li]
            kab_s.append(jnp.concatenate([ka, kb], axis=2))
            vab_s.append(jnp.concatenate([va, vb], axis=2))
            kidx_s.append(ki)
        else:
            q, k, v, logf = c_project(hp, w_in_c[li], b_forget[li])
            r_all = suffix_logf(logf)
            o = fox_attend(q, k, v, r_all, r_all, pos_p)
            xp = xp + merge_heads(o) @ w_out_c[li]
            kc_p.append(k)
            vc_p.append(v)
            lf_p.append(logf)
            q, k, v, logf = c_project(hs, w_in_c[li], b_forget[li])
            k_all = jnp.concatenate([paged_rows(cache_k_c, li, page_table, 0, H_C).astype(k.dtype), k], axis=1)
            v_all = jnp.concatenate([paged_rows(cache_v_c, li, page_table, 0, H_C).astype(v.dtype), v], axis=1)
            lf_past = cache_logf_c[li, page_table].reshape(dbsz, PAST_LEN, H_C).astype(jnp.float32)
            r_all = suffix_logf(jnp.concatenate([lf_past, logf], axis=1))
            o = fox_attend(q, k_all, v_all, r_all, r_all[:, PAST_LEN:], pos_s)
            xs = xs + merge_heads(o) @ w_out_c[li]
            kc_s.append(k)
            vc_s.append(v)
            lf_s.append(logf)
        y, st = conv_ffn(rmsnorm(xp, g_ffn[layer]), w_up[layer], conv_w[layer], conv_b[layer], w_down[layer],
                         jnp.zeros((bsz, CONV_W - 1, 2 * D_FF), xp.dtype))
        xp = xp + y
        cv_p.append(st)
        y, st = conv_ffn(rmsnorm(xs, g_ffn[layer]), w_up[layer], conv_w[layer], conv_b[layer], w_down[layer],
                         state_conv[layer])
        xs = xs + y
        cv_s.append(st)
    y_prompt = rmsnorm(xp, g_final)
    y_sample = rmsnorm(xs, g_final)
    return (y_prompt, y_sample,
            jnp.stack(kab_p), jnp.stack(vab_p), jnp.stack(kidx_p), jnp.stack(kc_p), jnp.stack(vc_p),
            jnp.stack(lf_p), jnp.stack(cv_p),
            jnp.stack(kab_s), jnp.stack(vab_s), jnp.stack(kidx_s), jnp.stack(kc_s), jnp.stack(vc_s),
            jnp.stack(lf_s), jnp.stack(cv_s))
```

```python
import functools
import math

import jax
import jax.numpy as jnp
import numpy as np
from jax import lax
from jax.experimental import pallas as pl
from jax.experimental.pallas import tpu as pltpu

F32 = jnp.float32
BF16 = jnp.bfloat16
I32 = jnp.int32

HEAD_DIM = 128
H_A = 8
H_B = 8
H_AB = H_A + H_B
H_C = 16
H_IDX = 4
D_IDX = 64
DSA_TOPK = 256
MOBA_BLOCK = 256
MOBA_TOPK = 3
NUM_BUCKETS = 32
MAX_DISTANCE = 128
CONV_W = 3
RMS_EPS = 1e-6
NEG_INF = -1e30
SCALE = HEAD_DIM ** -0.5
D_HEADS = H_AB * HEAD_DIM
TAIL_PAD = 384
LANE = 128
INT_MIN = -(2 ** 31)
VMEM_LIMIT = 56 * 1024 * 1024


def _cparams(n_axes):
    return pltpu.CompilerParams(dimension_semantics=("arbitrary",) * n_axes,
                                vmem_limit_bytes=VMEM_LIMIT)


def _rmsnorm_kernel(x_ref, g_ref, o_ref):
    x = x_ref[...]
    y = x * lax.rsqrt(jnp.mean(x * x, axis=-1, keepdims=True) + RMS_EPS)
    o_ref[...] = (y * g_ref[...]).astype(o_ref.dtype)


def _rmsnorm(x, g, out_dtype):
    m, d = x.shape
    tm = min(m, 512)
    return pl.pallas_call(
        _rmsnorm_kernel,
        out_shape=jax.ShapeDtypeStruct((m, d), out_dtype),
        grid=(m // tm,),
        in_specs=[pl.BlockSpec((tm, d), lambda i: (i, 0)),
                  pl.BlockSpec((1, d), lambda i: (0, 0))],
        out_specs=pl.BlockSpec((tm, d), lambda i: (i, 0)),
        compiler_params=_cparams(1),
        name="rmsnorm",
    )(x, g.reshape(1, d))


def _mm_kernel(*refs, has_res):
    if has_res:
        a_ref, w_ref, r_ref, o_ref, wbf = refs
    else:
        a_ref, w_ref, o_ref, wbf = refs

    @pl.when(pl.program_id(1) == 0)
    def _():
        wbf[...] = w_ref[...].astype(BF16)

    acc = jnp.dot(a_ref[...], wbf[...], preferred_element_type=F32)
    if has_res:
        acc = r_ref[...] + acc
    o_ref[...] = acc


def _matmul(a, w, layer, tn, res=None, tm=512):
    m, k = a.shape
    n = w.shape[-1]
    tm = min(m, tm)
    in_specs = [pl.BlockSpec((tm, k), lambda j, i: (i, 0)),
                pl.BlockSpec((None, k, tn), lambda j, i: (layer, 0, j))]
    args = [a, w]
    if res is not None:
        in_specs.append(pl.BlockSpec((tm, tn), lambda j, i: (i, j)))
        args.append(res)
    return pl.pallas_call(
        functools.partial(_mm_kernel, has_res=res is not None),
        out_shape=jax.ShapeDtypeStruct((m, n), F32),
        grid=(n // tn, m // tm),
        in_specs=in_specs,
        out_specs=pl.BlockSpec((tm, tn), lambda j, i: (i, j)),
        scratch_shapes=[pltpu.VMEM((k, tn), BF16)],
        compiler_params=_cparams(2),
        name="matmul",
    )(*args)


def _qkv_kernel(a_ref, wq_ref, wk_ref, wv_ref, *refs, with_kmean):
    if with_kmean:
        q_ref, k_ref, v_ref, kbf_ref, vbf_ref, km_ref, wbf = refs
    else:
        q_ref, k_ref, v_ref, kbf_ref, vbf_ref, wbf = refs

    @pl.when(pl.program_id(1) == 0)
    def _():
        wbf[0] = wq_ref[...].astype(BF16)
        wbf[1] = wk_ref[...].astype(BF16)
        wbf[2] = wv_ref[...].astype(BF16)

    a = a_ref[...]
    q_ref[...] = jnp.dot(a, wbf[0], preferred_element_type=F32)
    k = jnp.dot(a, wbf[1], preferred_element_type=F32)
    v = jnp.dot(a, wbf[2], preferred_element_type=F32)
    k_ref[...] = k
    v_ref[...] = v
    kbf_ref[...] = k.astype(BF16)
    vbf_ref[...] = v.astype(BF16)
    if with_kmean:
        tm, tn = k.shape
        km_ref[...] = jnp.mean(k.reshape(tm // MOBA_BLOCK, MOBA_BLOCK, tn), axis=1)


def _qkv_proj(a, w, layer, col_blocks, with_kmean, tn=256, tm=1024):
    m, kdim = a.shape
    tm = min(m, tm)
    nj = D_HEADS // tn

    def wspec(which):
        return pl.BlockSpec((None, kdim, tn), lambda j, i: (layer, 0, col_blocks(j)[which]))

    ospec = pl.BlockSpec((tm, tn), lambda j, i: (i, j))
    out_shape = [jax.ShapeDtypeStruct((m, D_HEADS), F32)] * 3 + [jax.ShapeDtypeStruct((m, D_HEADS), BF16)] * 2
    out_specs = [ospec] * 5
    if with_kmean:
        out_shape.append(jax.ShapeDtypeStruct((m // tm, tm // MOBA_BLOCK, D_HEADS), F32))
        out_specs.append(pl.BlockSpec((None, tm // MOBA_BLOCK, tn), lambda j, i: (i, 0, j)))
    return pl.pallas_call(
        functools.partial(_qkv_kernel, with_kmean=with_kmean),
        out_shape=out_shape,
        grid=(nj, m // tm),
        in_specs=[pl.BlockSpec((tm, kdim), lambda j, i: (i, 0)), wspec(0), wspec(1), wspec(2)],
        out_specs=out_specs,
        scratch_shapes=[pltpu.VMEM((3, kdim, tn), BF16)],
        compiler_params=_cparams(2),
        name="qkv_proj",
    )(a, w, w, w)


def _silu(x):
    return x * (1.0 / (1.0 + jnp.exp(-x)))


def _ffn_up_kernel(a_ref, ap_ref, wg_ref, wv_ref, cwg_ref, cwv_ref, cbg_ref, cbv_ref, pg_ref, pv_ref,
                   act_ref, sg_ref, sv_ref, wbf, *, tiles_per_seq, decode):
    i = pl.program_id(1)

    @pl.when(i == 0)
    def _():
        wbf[0] = wg_ref[...].astype(BF16)
        wbf[1] = wv_ref[...].astype(BF16)

    a = a_ref[...]
    tm = a.shape[0]
    row = lax.broadcasted_iota(I32, (tm, 1), 0)
    outs = []
    for half, (cw_ref, cb_ref, p_ref, s_ref) in enumerate(
            ((cwg_ref, cbg_ref, pg_ref, sg_ref), (cwv_ref, cbv_ref, pv_ref, sv_ref))):
        u = jnp.dot(a, wbf[half], preferred_element_type=F32)
        cw = cw_ref[...]
        if decode:
            p0 = p_ref[:, 0, :]
            p1 = p_ref[:, 1, :]
            c = cb_ref[...] + cw[0:1] * p0 + cw[1:2] * p1 + cw[2:3] * u
            s_ref[:, 0, :] = p1
            s_ref[:, 1, :] = u
        else:
            first = (i % tiles_per_seq) == 0
            up = jnp.dot(ap_ref[...], wbf[half], preferred_element_type=F32)
            prev = p_ref[0]
            m1 = jnp.where(first, prev[1:2], up[7:8])
            m2 = jnp.where(first, prev[0:1], up[6:7])
            r1 = pltpu.roll(u, 1, axis=0)
            r2 = pltpu.roll(u, 2, axis=0)
            u1 = jnp.where(row == 0, m1, r1)
            u2 = jnp.where(row == 0, m2, jnp.where(row == 1, m1, r2))
            c = cb_ref[...] + cw[0:1] * u2 + cw[1:2] * u1 + cw[2:3] * u

            @pl.when((i % tiles_per_seq) == tiles_per_seq - 1)
            def _():
                s_ref[0] = u[tm - 2:tm]
        outs.append(c)
    act_ref[...] = (_silu(outs[0]) * outs[1]).astype(BF16)


def _ffn_up(a, w_up, conv_w, conv_b3, prev, layer, seq_rows, tn=512, tm=512):
    m, d = a.shape
    d_ff = w_up.shape[-1] // 2
    n_seq = m // seq_rows
    decode = seq_rows == 1
    tm = m if decode else min(tm, seq_rows)
    tiles_per_seq = 1 if decode else seq_rows // tm
    nj = d_ff // tn
    pb = tm // 8

    if decode:
        pspec = lambda off: pl.BlockSpec((n_seq, 2, tn), lambda j, i: (0, 0, j + off))
        sspec = pl.BlockSpec((n_seq, 2, tn), lambda j, i: (0, 0, j))
    else:
        pspec = lambda off: pl.BlockSpec((1, 2, tn), lambda j, i: (i // tiles_per_seq, 0, j + off))
        sspec = pl.BlockSpec((1, 2, tn), lambda j, i: (i // tiles_per_seq, 0, j))
    in_specs = [
        pl.BlockSpec((tm, d), lambda j, i: (i, 0)),
        pl.BlockSpec((8, d), lambda j, i: (jnp.maximum(i * pb - 1, 0), 0)),
        pl.BlockSpec((None, d, tn), lambda j, i: (layer, 0, j)),
        pl.BlockSpec((None, d, tn), lambda j, i: (layer, 0, j + nj)),
        pl.BlockSpec((None, CONV_W, tn), lambda j, i: (layer, 0, j)),
        pl.BlockSpec((None, CONV_W, tn), lambda j, i: (layer, 0, j + nj)),
        pl.BlockSpec((None, 1, tn), lambda j, i: (layer, 0, j)),
        pl.BlockSpec((None, 1, tn), lambda j, i: (layer, 0, j + nj)),
        pspec(0), pspec(nj),
    ]
    return pl.pallas_call(
        functools.partial(_ffn_up_kernel, tiles_per_seq=tiles_per_seq, decode=decode),
        out_shape=[jax.ShapeDtypeStruct((m, d_ff), BF16),
                   jax.ShapeDtypeStruct((n_seq, 2, d_ff), F32),
                   jax.ShapeDtypeStruct((n_seq, 2, d_ff), F32)],
        grid=(nj, m // tm),
        in_specs=in_specs,
        out_specs=[pl.BlockSpec((tm, tn), lambda j, i: (i, j)), sspec, sspec],
        scratch_shapes=[pltpu.VMEM((2, d, tn), BF16)],
        compiler_params=_cparams(2),
        name="ffn_up",
    )(a, a, w_up, w_up, conv_w, conv_w, conv_b3, conv_b3, prev, prev)


def _t5_bucket(dist):
    dist = jnp.maximum(dist, 0)
    max_exact = NUM_BUCKETS // 2
    log_ratio = jnp.log(jnp.maximum(dist, 1).astype(F32) / max_exact) / math.log(MAX_DISTANCE / max_exact)
    large = jnp.minimum(max_exact + (log_ratio * (NUM_BUCKETS - max_exact)).astype(I32), NUM_BUCKETS - 1)
    return jnp.where(dist < max_exact, dist, large)


def _t5_tiles(t5_table):
    r = jnp.arange(LANE, dtype=I32)
    dist = jnp.arange(3, dtype=I32)[:, None, None] * LANE + r[None, :, None] - r[None, None, :]
    bias = jnp.moveaxis(t5_table[_t5_bucket(dist)], -1, 0).astype(F32)
    return jnp.where(dist[None] >= 0, bias, NEG_INF)


def _fold_lanes(x, op):
    acc = x[:, :LANE]
    for t in range(1, x.shape[1] // LANE):
        acc = op(acc, x[:, t * LANE:(t + 1) * LANE])
    return acc


def _softmax_attend(q_bf, k_ref, v_ref, ls_ref, n_plain, bias_fn, tk, n_masked=1):
    tq = q_bf.shape[0]

    def logits(c, macc, masked):
        off = pl.multiple_of(c * tk, tk)
        s = lax.dot_general(q_bf, k_ref[pl.ds(off, tk), :], (((1,), (1,)), ((), ())),
                            preferred_element_type=F32)
        s = s + bias_fn(c, masked)
        ls_ref[:, pl.ds(off, tk)] = s
        return jnp.maximum(macc, _fold_lanes(s, jnp.maximum))

    macc = lax.fori_loop(0, n_plain, lambda c, m: logits(c, m, False),
                         jnp.full((tq, LANE), -jnp.inf, F32))
    for t in range(n_masked):
        macc = logits(n_plain + t, macc, True)
    m = jnp.max(macc, axis=1, keepdims=True)

    def pv(c, carry):
        lacc, acc = carry
        off = pl.multiple_of(c * tk, tk)
        p = jnp.exp(ls_ref[:, pl.ds(off, tk)] - m)
        lacc = lacc + _fold_lanes(p, jnp.add)
        acc = acc + jnp.dot(p.astype(BF16), v_ref[pl.ds(off, tk), :], preferred_element_type=F32)
        return lacc, acc

    lacc, acc = lax.fori_loop(0, n_plain + n_masked, pv,
                              (jnp.zeros((tq, LANE), F32), jnp.zeros((tq, HEAD_DIM), F32)))
    return acc / jnp.sum(lacc, axis=1, keepdims=True)


def _logf_suffix_kernel(fg_ref, b_ref, lf_ref, r_ref):
    z = fg_ref[...] + b_ref[...]
    lf = jnp.minimum(z, 0.0) - jnp.log(1.0 + jnp.exp(-jnp.abs(z)))
    lf_ref[...] = lf
    n = lf.shape[1]
    lane = lax.broadcasted_iota(I32, lf.shape, 1)
    x = lf
    k = 1
    while k < n:
        x = x + jnp.where(lane >= k, pltpu.roll(x, k, axis=1), 0.0)
        k *= 2
    r_ref[...] = x[:, n - 1:n] - x


def _logf_suffix(fg_t, b_col):
    return pl.pallas_call(
        _logf_suffix_kernel,
        out_shape=[jax.ShapeDtypeStruct(fg_t.shape, F32)] * 2,
        compiler_params=pltpu.CompilerParams(vmem_limit_bytes=VMEM_LIMIT),
        name="logf_suffix",
    )(fg_t, b_col)


def _fox_kernel(q_ref, k_ref, v_ref, rk_ref, rq_ref, o_ref, ls_ref, *, tq):
    i = pl.program_id(2)
    q_bf = (q_ref[...] * SCALE).astype(BF16)
    rq = rq_ref[...]
    row = lax.broadcasted_iota(I32, (tq, tq), 0)
    col = lax.broadcasted_iota(I32, (tq, tq), 1)

    def bias_fn(c, masked):
        off = pl.multiple_of(c * tq, tq)
        b = rk_ref[:, pl.ds(off, tq)] - rq
        if masked:
            b = jnp.where(col <= row, b, NEG_INF)
        return b

    o = _softmax_attend(q_bf, k_ref, v_ref, ls_ref, i, bias_fn, tq)
    o_ref[...] = o.astype(o_ref.dtype)


def _fox_prompt(q, kbf, vbf, rk, rq, bsz, seq, tq=256):
    tq = min(tq, seq)
    nq = seq // tq
    return pl.pallas_call(
        functools.partial(_fox_kernel, tq=tq),
        out_shape=jax.ShapeDtypeStruct(q.shape, BF16),
        grid=(bsz, H_C, nq),
        in_specs=[pl.BlockSpec((tq, HEAD_DIM), lambda b, h, i: (b * nq + i, h)),
                  pl.BlockSpec((seq, HEAD_DIM), lambda b, h, i: (b, h)),
                  pl.BlockSpec((seq, HEAD_DIM), lambda b, h, i: (b, h)),
                  pl.BlockSpec((None, None, 1, seq), lambda b, h, i: (b, h, 0, 0)),
                  pl.BlockSpec((None, None, tq, 1), lambda b, h, i: (b, h, i, 0))],
        out_specs=pl.BlockSpec((tq, HEAD_DIM), lambda b, h, i: (b * nq + i, h)),
        scratch_shapes=[pltpu.VMEM((tq, seq), F32)],
        compiler_params=_cparams(3),
        name="fox_prompt",
    )(q, kbf, vbf, rk, rq)


def _ordered_key(score):
    score = jnp.where(score == 0.0, 0.0, score)
    bits = pltpu.bitcast(score, I32)
    return jnp.where(bits < 0, bits ^ 0x7FFFFFFF, bits)


def _kth_largest_key(count_ge, k, shape):
    def body(step, r_u):
        cand_u = r_u | lax.shift_left(jnp.int32(1), 31 - step)
        return jnp.where(count_ge(cand_u ^ INT_MIN) >= k, cand_u, r_u)

    return lax.fori_loop(0, 32, body, jnp.zeros(shape, I32)) ^ INT_MIN


def _dsa_kernel(q_ref, tail_ref, kidx_ref, k_ref, v_ref, tb_ref, o_ref, key_ref, mb_ref, ls_ref, *, tq, k_sel):
    i = pl.program_id(1)
    n_t = i + 1
    row = lax.broadcasted_iota(I32, (tq, LANE), 0)
    col = lax.broadcasted_iota(I32, (tq, LANE), 1)
    tail = tail_ref[...]
    qi = [tail[:, h * D_IDX:(h + 1) * D_IDX].astype(BF16) for h in range(H_IDX)]
    wi = [tail[:, H_IDX * D_IDX + D_IDX + h:H_IDX * D_IDX + D_IDX + h + 1] for h in range(H_IDX)]

    def score_tile(c, carry):
        off = pl.multiple_of(c * LANE, LANE)
        kt = kidx_ref[pl.ds(off, LANE), :].astype(BF16)
        score = jnp.zeros((tq, LANE), F32)
        for h in range(H_IDX):
            s = lax.dot_general(qi[h], kt, (((1,), (1,)), ((), ())), preferred_element_type=F32)
            score = score + jnp.maximum(s, 0.0) * wi[h]
        score = jnp.where(c * LANE + col <= i * tq + row, score, NEG_INF)
        key_ref[:, pl.ds(off, LANE)] = _ordered_key(score)
        return carry

    lax.fori_loop(0, n_t, score_tile, 0)

    def count(pred):
        def body(c, acc):
            off = pl.multiple_of(c * LANE, LANE)
            return acc + jnp.where(pred(key_ref[:, pl.ds(off, LANE)]), 1, 0)
        acc = lax.fori_loop(0, n_t, body, jnp.zeros((tq, LANE), I32))
        return jnp.sum(acc, axis=1, keepdims=True)

    thr = _kth_largest_key(lambda t: count(lambda key: key >= t), k_sel, (tq, 1))
    need = (k_sel - count(lambda key: key > thr)).astype(F32)
    incl = jnp.where(row <= col, 1.0, 0.0).astype(BF16)

    def mask_tile(c, seen):
        off = pl.multiple_of(c * LANE, LANE)
        key = key_ref[:, pl.ds(off, LANE)]
        eq = key == thr
        pre = jnp.dot(jnp.where(eq, 1.0, 0.0).astype(BF16), incl, preferred_element_type=F32)
        sel = (key > thr) | (eq & (pre + seen <= need))
        mb_ref[:, pl.ds(off, LANE)] = jnp.where(sel, 0.0, NEG_INF)
        return seen + pre[:, LANE - 1:LANE]

    lax.fori_loop(0, n_t, mask_tile, jnp.zeros((tq, 1), F32))

    for h in range(H_A):
        hs = slice(h * HEAD_DIM, (h + 1) * HEAD_DIM)
        q_bf = (q_ref[:, hs] * SCALE).astype(BF16)

        def bias_fn(c, masked, h=h):
            off = pl.multiple_of(c * LANE, LANE)
            return mb_ref[:, pl.ds(off, LANE)] + tb_ref[h, jnp.minimum(i - c, 2)]

        o = _softmax_attend(q_bf, k_ref.at[:, hs], v_ref.at[:, hs], ls_ref, n_t, bias_fn, LANE, n_masked=0)
        o_ref[:, hs] = o.astype(o_ref.dtype)


def _dsa_prompt(q, tail, kidx, kbf, vbf, tb, bsz, seq):
    tq = LANE
    nq = seq // tq
    wa = H_A * HEAD_DIM
    k_sel = min(DSA_TOPK, seq // 4)
    return pl.pallas_call(
        functools.partial(_dsa_kernel, tq=tq, k_sel=k_sel),
        out_shape=jax.ShapeDtypeStruct(q.shape, BF16),
        grid=(bsz, nq),
        in_specs=[pl.BlockSpec((tq, wa), lambda b, i: (b * nq + i, 0)),
                  pl.BlockSpec((tq, tail.shape[1]), lambda b, i: (b * nq + i, 0)),
                  pl.BlockSpec((seq, D_IDX), lambda b, i: (b, 0)),
                  pl.BlockSpec((seq, wa), lambda b, i: (b, 0), pipeline_mode=pl.Buffered(1)),
                  pl.BlockSpec((seq, wa), lambda b, i: (b, 0), pipeline_mode=pl.Buffered(1)),
                  pl.BlockSpec((H_A, 3, LANE, LANE), lambda b, i: (0, 0, 0, 0))],
        out_specs=pl.BlockSpec((tq, wa), lambda b, i: (b * nq + i, 0)),
        scratch_shapes=[pltpu.VMEM((tq, seq), I32), pltpu.VMEM((tq, seq), F32), pltpu.VMEM((tq, seq), F32)],
        compiler_params=_cparams(2),
        name="dsa_prompt",
    )(q, tail, kidx, kbf, vbf, tb)


def _top_blocks(gate, n_valid, n_top):
    nblk = gate.shape[1]
    blk = lax.broadcasted_iota(I32, gate.shape, 1)
    g = jnp.where(blk < n_valid, gate, NEG_INF)
    rank = jnp.zeros(gate.shape, I32)
    for m in range(nblk):
        gm = g[:, m:m + 1]
        rank = rank + jnp.where((gm > g) | ((gm == g) & (m < blk)), 1, 0)
    return jnp.where((blk < n_valid) & (rank < n_top), 1.0, 0.0)


def _moba_kernel(o_in_ref, q_ref, km_ref, k_ref, v_ref, tb_ref, o_ref, ls_ref, *, tq, n_top):
    del o_in_ref
    i = pl.program_id(2)
    own = (i * tq) // MOBA_BLOCK
    q = q_ref[...]
    gate = lax.dot_general(q, km_ref[...], (((1,), (1,)), ((), ())), preferred_element_type=F32,
                           precision=lax.Precision.HIGHEST)
    sel = _top_blocks(gate, own, n_top)
    blk = lax.broadcasted_iota(I32, sel.shape, 1)
    sel = jnp.where(blk == own, 1.0, sel)
    q_bf = (q * SCALE).astype(BF16)
    per_blk = MOBA_BLOCK // LANE

    def bias_fn(c, masked):
        picked = jnp.sum(jnp.where(blk == c // per_blk, sel, 0.0), axis=1, keepdims=True)
        return jnp.where(picked > 0.5, 0.0, NEG_INF) + tb_ref[0, jnp.minimum(i - c, 2)]

    o = _softmax_attend(q_bf, k_ref, v_ref, ls_ref, i + 1, bias_fn, LANE, n_masked=0)
    o_ref[...] = o.astype(o_ref.dtype)


def _moba_prompt(o_ab, q, kmean, kbf, vbf, tb, bsz, seq):
    tq = LANE
    nq = seq // tq
    nblk = kmean.shape[1]
    n_top = min(MOBA_TOPK, nblk)
    return pl.pallas_call(
        functools.partial(_moba_kernel, tq=tq, n_top=n_top),
        out_shape=jax.ShapeDtypeStruct(o_ab.shape, o_ab.dtype),
        grid=(bsz, H_B, nq),
        in_specs=[pl.BlockSpec(memory_space=pl.ANY),
                  pl.BlockSpec((tq, HEAD_DIM), lambda b, h, i: (b * nq + i, H_A + h)),
                  pl.BlockSpec((None, nblk, HEAD_DIM), lambda b, h, i: (b, 0, H_A + h)),
                  pl.BlockSpec((seq, HEAD_DIM), lambda b, h, i: (b, H_A + h)),
                  pl.BlockSpec((seq, HEAD_DIM), lambda b, h, i: (b, H_A + h)),
                  pl.BlockSpec((1, 3, LANE, LANE), lambda b, h, i: (H_A + h, 0, 0, 0))],
        out_specs=pl.BlockSpec((tq, HEAD_DIM), lambda b, h, i: (b * nq + i, H_A + h)),
        scratch_shapes=[pltpu.VMEM((tq, seq), F32)],
        input_output_aliases={0: 0},
        compiler_params=_cparams(3),
        name="moba_prompt",
    )(o_ab, q, kmean, kbf, vbf, tb)


def _paged_specs(n, block, layer, pages_of):
    def one(r):
        zeros = (0,) * (len(block) - 2)
        return pl.BlockSpec(block, lambda b, g, pt: (layer, pt[b, pages_of(b, g, r)]) + zeros)
    return [one(r) for r in range(n)]


def _idx_scores_kernel(pt_ref, *refs, n_pg):
    del pt_ref
    kp, (qi_ref, wi_ref, o_ref) = refs[:n_pg], refs[n_pg:]
    qi = qi_ref[...].astype(BF16)
    wi = wi_ref[...]
    for r in range(n_pg):
        s = lax.dot_general(qi, kp[r][...].astype(BF16), (((1,), (1,)), ((), ())), preferred_element_type=F32)
        o_ref[r:r + 1, :] = jnp.sum(jnp.maximum(s, 0.0) * wi, axis=0, keepdims=True)


def _idx_scores_sample(cache_kidx, li, page_table, qi8, wi8):
    nb, n_pages = page_table.shape
    page = cache_kidx.shape[2]
    n_pg = min(16, n_pages)
    blk = (None, None, page, D_IDX)
    return pl.pallas_call(
        functools.partial(_idx_scores_kernel, n_pg=n_pg),
        out_shape=jax.ShapeDtypeStruct((nb, n_pages, page), F32),
        grid_spec=pltpu.PrefetchScalarGridSpec(
            num_scalar_prefetch=1,
            grid=(nb, n_pages // n_pg),
            in_specs=_paged_specs(n_pg, blk, li, lambda b, g, r: g * n_pg + r)
            + [pl.BlockSpec((None, 8, D_IDX), lambda b, g, pt: (b, 0, 0)),
               pl.BlockSpec((None, 8, 1), lambda b, g, pt: (b, 0, 0))],
            out_specs=pl.BlockSpec((None, n_pg, page), lambda b, g, pt: (b, g, 0))),
        compiler_params=_cparams(2),
        name="idx_scores_sample",
    )(page_table, *([cache_kidx] * n_pg), qi8, wi8)


def _dsa_select_kernel(sc_ref, qi_ref, wi_ref, kin_ref, mb_ref, mbn_ref, *, k_sel):
    nb, n_pages, page = sc_ref.shape
    incl = jnp.where(lax.broadcasted_iota(I32, (page, page), 0) <= lax.broadcasted_iota(I32, (page, page), 1),
                     1.0, 0.0).astype(BF16)
    before = jnp.where(lax.broadcasted_iota(I32, (n_pages, n_pages), 1) < lax.broadcasted_iota(I32, (n_pages, n_pages), 0),
                       1.0, 0.0).astype(BF16)

    def total(x):
        return jnp.sum(jnp.sum(x, axis=0, keepdims=True), axis=1, keepdims=True)

    for b in range(nb):
        s_new = jnp.sum(qi_ref[b] * kin_ref[b:b + 1, :], axis=1, keepdims=True)
        s_new = jnp.sum(jnp.maximum(s_new, 0.0) * wi_ref[b], axis=0, keepdims=True)
        key = _ordered_key(sc_ref[b])
        key_new = _ordered_key(jnp.broadcast_to(s_new, (8, LANE)))[0:1, 0:1]

        def count_ge(t):
            return total(jnp.where(key >= t, 1, 0)) + jnp.where(key_new >= t, 1, 0)

        thr = _kth_largest_key(count_ge, k_sel, (1, 1))
        gt = key > thr
        need = (k_sel - total(jnp.where(gt, 1, 0)) - jnp.where(key_new > thr, 1, 0)).astype(F32)
        eq = key == thr
        eq_bf = jnp.where(eq, 1.0, 0.0).astype(BF16)
        pre = jnp.dot(eq_bf, incl, preferred_element_type=F32)
        off = jnp.sum(jnp.dot(before, eq_bf, preferred_element_type=F32), axis=1, keepdims=True)
        sel = gt | (eq & (pre + off <= need))
        mb_ref[b] = jnp.where(sel, 0.0, NEG_INF)
        n_eq = total(jnp.where(eq, 1.0, 0.0))
        sel_new = (key_new > thr) | ((key_new == thr) & (n_eq + 1.0 <= need))
        mbn_ref[b:b + 1, :] = jnp.broadcast_to(jnp.where(sel_new, 0.0, NEG_INF), (1, LANE))


def _dsa_select_sample(scores, qi8, wi8, ki_new):
    nb, n_pages, page = scores.shape
    k_sel = min(DSA_TOPK, (n_pages * page + 1) // 4)
    return pl.pallas_call(
        functools.partial(_dsa_select_kernel, k_sel=k_sel),
        out_shape=[jax.ShapeDtypeStruct(scores.shape, F32), jax.ShapeDtypeStruct((nb, LANE), F32)],
        compiler_params=pltpu.CompilerParams(vmem_limit_bytes=VMEM_LIMIT),
        name="dsa_select_sample",
    )(scores, qi8, wi8, ki_new)


def _logf_suffix_sample_kernel(pt_ref, *refs, n_pg):
    del pt_ref
    lp, (fg_ref, bf_ref, r_ref, lfn_ref, carry) = refs[:n_pg], refs[n_pg:]
    page = lp[0].shape[0]

    @pl.when(pl.program_id(1) == 0)
    def _():
        z = fg_ref[...] + bf_ref[...]
        lfn = jnp.minimum(z, 0.0) - jnp.log(1.0 + jnp.exp(-jnp.abs(z)))
        lfn_ref[...] = lfn
        carry[...] = lfn

    row = lax.broadcasted_iota(I32, (page, H_C), 0)
    c = carry[...]
    for r in reversed(range(n_pg)):
        lf = lp[r][...]
        x = lf
        k = 1
        while k < page:
            x = x + jnp.where(row + k < page, pltpu.roll(x, page - k, axis=0), 0.0)
            k *= 2
        r_ref[r * page:(r + 1) * page, :] = x - lf + c
        c = c + x[0:1, :]
    carry[...] = c


def _logf_suffix_sample(cache_logf, li, page_table, fg_new, b_f):
    nb, n_pages = page_table.shape
    page = cache_logf.shape[2]
    n_pg = min(16, n_pages)
    n_steps = n_pages // n_pg
    blk = (None, None, page, H_C)
    return pl.pallas_call(
        functools.partial(_logf_suffix_sample_kernel, n_pg=n_pg),
        out_shape=[jax.ShapeDtypeStruct((nb, n_pages * page, H_C), F32), jax.ShapeDtypeStruct((nb, 1, H_C), F32)],
        grid_spec=pltpu.PrefetchScalarGridSpec(
            num_scalar_prefetch=1,
            grid=(nb, n_steps),
            in_specs=_paged_specs(n_pg, blk, li, lambda b, g, r: (n_steps - 1 - g) * n_pg + r)
            + [pl.BlockSpec((None, 1, H_C), lambda b, g, pt: (b, 0, 0)),
               pl.BlockSpec((1, H_C), lambda b, g, pt: (0, 0))],
            out_specs=[pl.BlockSpec((None, n_pg * page, H_C), lambda b, g, pt: (b, n_steps - 1 - g, 0)),
                       pl.BlockSpec((None, 1, H_C), lambda b, g, pt: (b, 0, 0))],
            scratch_shapes=[pltpu.VMEM((1, H_C), F32)]),
        compiler_params=_cparams(2),
        name="logf_suffix_sample",
    )(page_table, *([cache_logf] * n_pg), fg_new, b_f)


def _decode_partials_kernel(pt_ref, *refs, n_pg, has_mask):
    del pt_ref
    kp, vp, rest = refs[:n_pg], refs[n_pg:2 * n_pg], refs[2 * n_pg:]
    if has_mask:
        qbd_ref, ebd_ref, bias_ref, mask_ref, m_ref, l_ref, acc_ref, ks_ref, kbf = rest
    else:
        qbd_ref, ebd_ref, bias_ref, m_ref, l_ref, acc_ref, ks_ref, kbf = rest
    page = kp[0].shape[0]
    for r in range(n_pg):
        kbf[r * page:(r + 1) * page, :] = kp[r][...].astype(BF16)
    logits = jnp.dot(kbf[...], qbd_ref[...], preferred_element_type=F32)
    s = logits[:, :H_AB] + bias_ref[...]
    if has_mask:
        lane = lax.broadcasted_iota(I32, s.shape, 1)
        s = s + jnp.where(lane < H_A, mask_ref[...], 0.0)
    per_blk = MOBA_BLOCK // page
    for blk in range(n_pg // per_blk):
        rows = slice(blk * MOBA_BLOCK, (blk + 1) * MOBA_BLOCK)
        sb = s[rows]
        m = jnp.max(sb, axis=0, keepdims=True)
        p = jnp.exp(sb - m)
        m_ref[blk:blk + 1, :] = m
        l_ref[blk:blk + 1, :] = jnp.sum(p, axis=0, keepdims=True)
        pexp = jnp.dot(p.astype(BF16), ebd_ref[...], preferred_element_type=F32)
        acc = jnp.zeros((1, D_HEADS), F32)
        ks = jnp.zeros((1, D_HEADS), F32)
        for r in range(per_blk):
            pg = blk * per_blk + r
            acc = acc + jnp.sum(pexp[r * page:(r + 1) * page] * vp[pg][...], axis=0, keepdims=True)
            ks = ks + jnp.sum(kp[pg][...], axis=0, keepdims=True)
        acc_ref[blk:blk + 1, :] = acc
        ks_ref[blk:blk + 1, :] = ks


def _decode_partials(cache_k, cache_v, li, page_table, qbd, ebd, bias, mask):
    nb, n_pages = page_table.shape
    page = cache_k.shape[2]
    n_pg = 2 * (MOBA_BLOCK // page)
    n_steps = n_pages // n_pg
    nblk = n_pg * page // MOBA_BLOCK
    rows = n_pg * page
    blk = (None, None, page, D_HEADS)
    pages_of = lambda b, g, r: g * n_pg + r
    per_b = bias.shape[0] > 1
    in_specs = (_paged_specs(n_pg, blk, li, pages_of) + _paged_specs(n_pg, blk, li, pages_of)
                + [pl.BlockSpec((None, D_HEADS, LANE), lambda b, g, pt: (b, 0, 0)),
                   pl.BlockSpec((H_AB, D_HEADS), lambda b, g, pt: (0, 0)),
                   pl.BlockSpec((None, rows, H_AB), lambda b, g, pt: (b if per_b else 0, g, 0))])
    args = [page_table] + [cache_k] * n_pg + [cache_v] * n_pg + [qbd, ebd, bias]
    if mask is not None:
        in_specs.append(pl.BlockSpec((None, rows, 1), lambda b, g, pt: (b, g, 0)))
        args.append(mask)
    small = pl.BlockSpec((None, None, nblk, H_AB), lambda b, g, pt: (b, g, 0, 0))
    wide = pl.BlockSpec((None, None, nblk, D_HEADS), lambda b, g, pt: (b, g, 0, 0))
    m, l, acc, ks = pl.pallas_call(
        functools.partial(_decode_partials_kernel, n_pg=n_pg, has_mask=mask is not None),
        out_shape=[jax.ShapeDtypeStruct((nb, n_steps, nblk, H_AB), F32)] * 2
        + [jax.ShapeDtypeStruct((nb, n_steps, nblk, D_HEADS), F32)] * 2,
        grid_spec=pltpu.PrefetchScalarGridSpec(
            num_scalar_prefetch=1,
            grid=(nb, n_steps),
            in_specs=in_specs,
            out_specs=[small, small, wide, wide],
            scratch_shapes=[pltpu.VMEM((rows, D_HEADS), BF16)]),
        compiler_params=_cparams(2),
        name="decode_partials",
    )(*args)
    nb_all = n_steps * nblk
    return (m.reshape(nb, nb_all, H_AB), l.reshape(nb, nb_all, H_AB),
            acc.reshape(nb, nb_all, D_HEADS), ks.reshape(nb, nb_all, D_HEADS))


def _decode_combine_kernel(m_ref, l_ref, acc_ref, ks_ref, q_ref, kn_ref, vn_ref, bn_ref, o_ref, *, moba_from, n_top):
    q = q_ref[...]
    prod = q * kn_ref[...]
    nblk = m_ref.shape[0]
    blk = lax.broadcasted_iota(I32, (nblk, 1), 0)
    for h in range(H_AB):
        hs = slice(h * HEAD_DIM, (h + 1) * HEAD_DIM)
        lg_new = jnp.sum(prod[:, hs], axis=1, keepdims=True) * SCALE + bn_ref[:, h:h + 1]
        m_h = m_ref[:, h:h + 1]
        if h >= moba_from:
            gate = jnp.sum(ks_ref[:, hs] * (1.0 / MOBA_BLOCK) * q[:, hs], axis=1, keepdims=True)
            keep = jnp.zeros((nblk, 1), jnp.bool_)
            for _ in range(n_top):
                best = jnp.max(gate, axis=0, keepdims=True)
                first = jnp.min(jnp.where(gate == best, blk, nblk), axis=0, keepdims=True)
                keep = keep | (blk == first)
                gate = jnp.where(blk == first, -jnp.inf, gate)
            m_h = jnp.where(keep, m_h, -jnp.inf)
        top = jnp.maximum(jnp.max(m_h, axis=0, keepdims=True), lg_new)
        w = jnp.exp(m_h - top)
        e_new = jnp.exp(lg_new - top)
        den = jnp.sum(w * l_ref[:, h:h + 1], axis=0, keepdims=True) + e_new
        num = jnp.sum(w * acc_ref[:, hs], axis=0, keepdims=True) + e_new * vn_ref[:, hs]
        o_ref[:, hs] = (num / den).astype(o_ref.dtype)


def _decode_combine(m, l, acc, ks, q, k_new, v_new, bias_new, moba_from):
    nb, nblk, _ = m.shape
    n_top = min(MOBA_TOPK, nblk)
    small = pl.BlockSpec((None, nblk, H_AB), lambda b: (b, 0, 0))
    wide = pl.BlockSpec((None, nblk, D_HEADS), lambda b: (b, 0, 0))
    row = pl.BlockSpec((None, 1, D_HEADS), lambda b: (b, 0, 0))
    out = pl.pallas_call(
        functools.partial(_decode_combine_kernel, moba_from=moba_from, n_top=n_top),
        out_shape=jax.ShapeDtypeStruct((nb, 1, D_HEADS), BF16),
        grid=(nb,),
        in_specs=[small, small, wide, wide, row, row, row, pl.BlockSpec((None, 1, LANE), lambda b: (b, 0, 0))],
        out_specs=row,
        compiler_params=_cparams(1),
        name="decode_combine",
    )(m, l, acc, ks, q.reshape(nb, 1, D_HEADS), k_new.reshape(nb, 1, D_HEADS), v_new.reshape(nb, 1, D_HEADS),
      bias_new.reshape(nb, 1, LANE))
    return out.reshape(nb, D_HEADS)


def _ab_cols(tn):
    per = H_A * HEAD_DIM // tn
    def cols(j):
        base = (j // per) * 3 * per + j % per
        return base, base + per, base + 2 * per
    return cols


def _c_cols(tn):
    per = D_HEADS // tn
    return lambda j: (j, j + per, j + 2 * per)


def _block_diag_queries(q):
    head_of_row = jnp.arange(D_HEADS, dtype=I32) // HEAD_DIM
    onehot = (head_of_row[:, None] == jnp.arange(LANE, dtype=I32)[None, :]).astype(F32)
    return ((q * SCALE)[:, :, None] * onehot[None]).astype(BF16)


def kernel(x_prompt, x_sample, cache_k_ab, cache_v_ab, cache_kidx, cache_k_c, cache_v_c, cache_logf_c, state_conv,
           page_table, g_mix, g_ffn, g_final, w_in_ab, w_out_ab, t5_table, w_in_c, b_forget, w_out_c, w_up, conv_w,
           conv_b, w_down):
    bsz, seq, d = x_prompt.shape
    nb, dseq, _ = x_sample.shape
    depth = g_mix.shape[0]
    d_ff = w_down.shape[1]
    n_pages = page_table.shape[1]
    page = cache_k_ab.shape[2]
    past = n_pages * page
    n_phys = cache_k_ab.shape[1]
    assert dseq == 1 and d == D_HEADS and seq % MOBA_BLOCK == 0 and past % MOBA_BLOCK == 0
    t_p = bsz * seq
    tn_ffn = 512 if d_ff % 512 == 0 else LANE

    qkv_w = 3 * D_HEADS
    n_tail_ab = w_in_ab.shape[-1] - qkv_w
    n_tail_c = w_in_c.shape[-1] - qkv_w
    w_tail_ab = jnp.pad(w_in_ab[:, :, qkv_w:], ((0, 0), (0, 0), (0, TAIL_PAD - n_tail_ab)))
    w_tail_c = jnp.pad(w_in_c[:, :, qkv_w:], ((0, 0), (0, 0), (0, LANE - n_tail_c)))
    conv_b3 = conv_b.reshape(depth, 1, 2 * d_ff)
    ck_ab = cache_k_ab.reshape(cache_k_ab.shape[0], n_phys, page, D_HEADS)
    cv_ab = cache_v_ab.reshape(cache_v_ab.shape[0], n_phys, page, D_HEADS)
    ck_c = cache_k_c.reshape(cache_k_c.shape[0], n_phys, page, D_HEADS)
    cv_c = cache_v_c.reshape(cache_v_c.shape[0], n_phys, page, D_HEADS)

    tb = _t5_tiles(t5_table)
    t5_past = t5_table[_t5_bucket(past - jnp.arange(past, dtype=I32))].astype(F32)[None]
    t5_self = jnp.pad(t5_table[_t5_bucket(jnp.zeros((), I32))].astype(F32), (0, LANE - H_AB))[None]
    lane = jnp.arange(LANE, dtype=I32)[None]
    ebd = (jnp.arange(H_AB, dtype=I32)[:, None] == (jnp.arange(D_HEADS, dtype=I32) // HEAD_DIM)[None, :]).astype(BF16)
    zero_conv = jnp.zeros((bsz, CONV_W - 1, 2 * d_ff), F32)
    iq, ik, iw = H_IDX * D_IDX, H_IDX * D_IDX + D_IDX, H_IDX * D_IDX + D_IDX + H_IDX

    xp = x_prompt.reshape(t_p, d)
    xs = x_sample.reshape(nb, d)
    outs = {name: [] for name in ("kab_p", "vab_p", "kidx_p", "kc_p", "vc_p", "lf_p", "cv_p",
                                  "kab_s", "vab_s", "kidx_s", "kc_s", "vc_s", "lf_s", "cv_s")}
    for layer in range(depth):
        li = layer // 2
        hp = _rmsnorm(xp, g_mix[layer], BF16)
        hs = _rmsnorm(xs, g_mix[layer], BF16)
        if layer % 2 == 0:
            q, k, v, kbf, vbf, kmean = _qkv_proj(hp, w_in_ab, li, _ab_cols(256), True)
            tail = _matmul(hp, w_tail_ab, li, TAIL_PAD)
            kidx = tail[:, iq:ik]
            o = _dsa_prompt(q, tail, kidx, kbf, vbf, tb, bsz, seq)
            o = _moba_prompt(o, q, kmean.reshape(bsz, seq // MOBA_BLOCK, D_HEADS), kbf, vbf, tb, bsz, seq)
            xp = _matmul(o, w_out_ab, li, 512, res=xp)
            outs["kab_p"].append(k.reshape(bsz, seq, H_AB, HEAD_DIM))
            outs["vab_p"].append(v.reshape(bsz, seq, H_AB, HEAD_DIM))
            outs["kidx_p"].append(kidx.reshape(bsz, seq, D_IDX))
            q, k, v, _, _ = _qkv_proj(hs, w_in_ab, li, _ab_cols(256), False)
            tail = _matmul(hs, w_tail_ab, li, TAIL_PAD)
            qi8 = jnp.pad(tail[:, :iq].reshape(nb, H_IDX, D_IDX), ((0, 0), (0, 8 - H_IDX), (0, 0)))
            wi8 = jnp.pad(tail[:, ik:iw].reshape(nb, H_IDX, 1), ((0, 0), (0, 8 - H_IDX), (0, 0)))
            ki_new = tail[:, iq:ik]
            scores = _idx_scores_sample(cache_kidx, li, page_table, qi8, wi8)
            mb, mb_new = _dsa_select_sample(scores, qi8, wi8, ki_new)
            parts = _decode_partials(ck_ab, cv_ab, li, page_table, _block_diag_queries(q), ebd, t5_past,
                                     mb.reshape(nb, past, 1))
            bias_new = t5_self + jnp.where(lane < H_A, mb_new, 0.0)
            o = _decode_combine(*parts, q, k, v, bias_new, H_A)
            xs = _matmul(o, w_out_ab, li, 512, res=xs)
            outs["kab_s"].append(k.reshape(nb, 1, H_AB, HEAD_DIM))
            outs["vab_s"].append(v.reshape(nb, 1, H_AB, HEAD_DIM))
            outs["kidx_s"].append(ki_new.reshape(nb, 1, D_IDX))
        else:
            q, k, v, kbf, vbf = _qkv_proj(hp, w_in_c, li, _c_cols(256), False)
            tail = _matmul(hp, w_tail_c, li, LANE)
            fg_t = jnp.transpose(tail[:, :H_C].reshape(bsz, seq, H_C), (0, 2, 1)).reshape(bsz * H_C, seq)
            lf_t, r_t = _logf_suffix(fg_t, jnp.tile(b_forget[li], bsz).reshape(bsz * H_C, 1))
            o = _fox_prompt(q, kbf, vbf, r_t.reshape(bsz, H_C, 1, seq), r_t.reshape(bsz, H_C, seq, 1), bsz, seq)
            xp = _matmul(o, w_out_c, li, 512, res=xp)
            outs["kc_p"].append(k.reshape(bsz, seq, H_C, HEAD_DIM))
            outs["vc_p"].append(v.reshape(bsz, seq, H_C, HEAD_DIM))
            outs["lf_p"].append(jnp.transpose(lf_t.reshape(bsz, H_C, seq), (0, 2, 1)))
            q, k, v, _, _ = _qkv_proj(hs, w_in_c, li, _c_cols(256), False)
            tail = _matmul(hs, w_tail_c, li, LANE)
            r_past, lf_new = _logf_suffix_sample(cache_logf_c, li, page_table, tail[:, :H_C].reshape(nb, 1, H_C),
                                                 b_forget[li].reshape(1, H_C))
            parts = _decode_partials(ck_c, cv_c, li, page_table, _block_diag_queries(q), ebd, r_past, None)
            o = _decode_combine(*parts, q, k, v, jnp.zeros((nb, LANE), F32), H_C)
            xs = _matmul(o, w_out_c, li, 512, res=xs)
            outs["kc_s"].append(k.reshape(nb, 1, H_C, HEAD_DIM))
            outs["vc_s"].append(v.reshape(nb, 1, H_C, HEAD_DIM))
            outs["lf_s"].append(lf_new)
        act, sg, sv = _ffn_up(_rmsnorm(xp, g_ffn[layer], BF16), w_up, conv_w, conv_b3, zero_conv, layer, seq, tn=tn_ffn)
        xp = _matmul(act, w_down, layer, 512, res=xp)
        outs["cv_p"].append(jnp.concatenate([sg, sv], axis=-1))
        act, sg, sv = _ffn_up(_rmsnorm(xs, g_ffn[layer], BF16), w_up, conv_w, conv_b3, state_conv[layer], layer, 1,
                              tn=tn_ffn)
        xs = _matmul(act, w_down, layer, 512, res=xs)
        outs["cv_s"].append(jnp.concatenate([sg, sv], axis=-1))
    y_prompt = _rmsnorm(xp, g_final, F32).reshape(bsz, seq, d)
    y_sample = _rmsnorm(xs, g_final, F32).reshape(nb, 1, d)
    st = lambda name: jnp.stack(outs[name])
    return (y_prompt, y_sample, st("kab_p"), st("vab_p"), st("kidx_p"), st("kc_p"), st("vc_p"), st("lf_p"), st("cv_p"),
            st("kab_s"), st("vab_s"), st("kidx_s"), st("kc_s"), st("vc_s"), st("lf_s"), st("cv_s"))
```

```python
import functools
import math

import jax
import jax.numpy as jnp
import numpy as np
from jax import lax
from jax.experimental import pallas as pl
from jax.experimental.pallas import tpu as pltpu

F32 = jnp.float32
BF16 = jnp.bfloat16
I32 = jnp.int32

HEAD_DIM = 128
H_A = 8
H_B = 8
H_AB = H_A + H_B
H_C = 16
H_IDX = 4
D_IDX = 64
DSA_TOPK = 256
MOBA_BLOCK = 256
MOBA_TOPK = 3
NUM_BUCKETS = 32
MAX_DISTANCE = 128
CONV_W = 3
RMS_EPS = 1e-6
NEG_INF = -1e30
SCALE = HEAD_DIM ** -0.5
D_HEADS = H_AB * HEAD_DIM
TAIL_PAD = 384
LANE = 128
CH = 512
SUB = CH // LANE
INT_MIN = -(2 ** 31)
VMEM_LIMIT = 56 * 1024 * 1024


def _cparams(n_axes):
    return pltpu.CompilerParams(dimension_semantics=("arbitrary",) * n_axes,
                                vmem_limit_bytes=VMEM_LIMIT)


def _rmsnorm_kernel(x_ref, g_ref, o_ref):
    x = x_ref[...]
    y = x * lax.rsqrt(jnp.mean(x * x, axis=-1, keepdims=True) + RMS_EPS)
    o_ref[...] = (y * g_ref[...]).astype(o_ref.dtype)


def _rmsnorm(x, g, out_dtype):
    m, d = x.shape
    tm = min(m, 512)
    return pl.pallas_call(
        _rmsnorm_kernel,
        out_shape=jax.ShapeDtypeStruct((m, d), out_dtype),
        grid=(m // tm,),
        in_specs=[pl.BlockSpec((tm, d), lambda i: (i, 0)),
                  pl.BlockSpec((1, d), lambda i: (0, 0))],
        out_specs=pl.BlockSpec((tm, d), lambda i: (i, 0)),
        compiler_params=_cparams(1),
        name="rmsnorm",
    )(x, g.reshape(1, d))


def _mm_kernel(*refs, has_res):
    if has_res:
        a_ref, w_ref, r_ref, o_ref, wbf = refs
    else:
        a_ref, w_ref, o_ref, wbf = refs

    @pl.when(pl.program_id(1) == 0)
    def _():
        wbf[...] = w_ref[...].astype(BF16)

    acc = jnp.dot(a_ref[...], wbf[...], preferred_element_type=F32)
    if has_res:
        acc = r_ref[...] + acc
    o_ref[...] = acc


def _matmul(a, w, layer, tn, res=None, tm=512):
    m, k = a.shape
    n = w.shape[-1]
    tm = min(m, tm)
    in_specs = [pl.BlockSpec((tm, k), lambda j, i: (i, 0)),
                pl.BlockSpec((None, k, tn), lambda j, i: (layer, 0, j))]
    args = [a, w]
    if res is not None:
        in_specs.append(pl.BlockSpec((tm, tn), lambda j, i: (i, j)))
        args.append(res)
    return pl.pallas_call(
        functools.partial(_mm_kernel, has_res=res is not None),
        out_shape=jax.ShapeDtypeStruct((m, n), F32),
        grid=(n // tn, m // tm),
        in_specs=in_specs,
        out_specs=pl.BlockSpec((tm, tn), lambda j, i: (i, j)),
        scratch_shapes=[pltpu.VMEM((k, tn), BF16)],
        compiler_params=_cparams(2),
        name="matmul",
    )(*args)


def _qkv_kernel(a_ref, wq_ref, wk_ref, wv_ref, *refs, with_kmean, n_alias):
    refs = refs[n_alias:]
    if with_kmean:
        q_ref, k_ref, v_ref, kbf_ref, vbf_ref, km_ref, wbf = refs
    else:
        q_ref, k_ref, v_ref, kbf_ref, vbf_ref, wbf = refs

    @pl.when(pl.program_id(1) == 0)
    def _():
        wbf[0] = wq_ref[...].astype(BF16)
        wbf[1] = wk_ref[...].astype(BF16)
        wbf[2] = wv_ref[...].astype(BF16)

    a = a_ref[...]
    q_ref[...] = jnp.dot(a, wbf[0], preferred_element_type=F32)
    k = jnp.dot(a, wbf[1], preferred_element_type=F32)
    v = jnp.dot(a, wbf[2], preferred_element_type=F32)
    k_ref[...] = k
    v_ref[...] = v
    kbf_ref[...] = k.astype(BF16)
    vbf_ref[...] = v.astype(BF16)
    if with_kmean:
        tm, tn = k.shape
        km_ref[...] = jnp.mean(k.reshape(tm // MOBA_BLOCK, MOBA_BLOCK, tn), axis=1)


def _qkv_proj(a, w, layer, col_blocks, with_kmean, n_stack, stacks=None, tn=256, tm=1024):
    m, kdim = a.shape
    tm = min(m, tm)
    nj = D_HEADS // tn

    def wspec(which):
        return pl.BlockSpec((None, kdim, tn), lambda j, i: (layer, 0, col_blocks(j)[which]))

    ospec = pl.BlockSpec((tm, tn), lambda j, i: (i, j))
    sspec = pl.BlockSpec((None, tm, tn), lambda j, i: (layer, i, j))
    flat = jax.ShapeDtypeStruct((m, D_HEADS), F32)
    stacked = jax.ShapeDtypeStruct((n_stack, m, D_HEADS), F32)
    out_shape = [flat, stacked, stacked] + [jax.ShapeDtypeStruct((m, D_HEADS), BF16)] * 2
    out_specs = [ospec, sspec, sspec, ospec, ospec]
    if with_kmean:
        out_shape.append(jax.ShapeDtypeStruct((m // tm, tm // MOBA_BLOCK, D_HEADS), F32))
        out_specs.append(pl.BlockSpec((None, tm // MOBA_BLOCK, tn), lambda j, i: (i, 0, j)))
    in_specs = [pl.BlockSpec((tm, kdim), lambda j, i: (i, 0)), wspec(0), wspec(1), wspec(2)]
    args = [a, w, w, w]
    aliases = {}
    if stacks is not None:
        in_specs += [pl.BlockSpec(memory_space=pl.ANY)] * 2
        args += list(stacks)
        aliases = {4: 1, 5: 2}
    return pl.pallas_call(
        functools.partial(_qkv_kernel, with_kmean=with_kmean, n_alias=len(aliases)),
        out_shape=out_shape,
        grid=(nj, m // tm),
        in_specs=in_specs,
        out_specs=out_specs,
        scratch_shapes=[pltpu.VMEM((3, kdim, tn), BF16)],
        input_output_aliases=aliases,
        compiler_params=_cparams(2),
        name="qkv_proj",
    )(*args)


def _silu(x):
    return x * (1.0 / (1.0 + jnp.exp(-x)))


def _ffn_up_kernel(a_ref, ap_ref, wg_ref, wv_ref, cwg_ref, cwv_ref, cbg_ref, cbv_ref, pg_ref, pv_ref,
                   act_ref, sg_ref, sv_ref, wbf, *, tiles_per_seq, decode):
    i = pl.program_id(1)

    @pl.when(i == 0)
    def _():
        wbf[0] = wg_ref[...].astype(BF16)
        wbf[1] = wv_ref[...].astype(BF16)

    a = a_ref[...]
    tm = a.shape[0]
    row = lax.broadcasted_iota(I32, (tm, 1), 0)
    outs = []
    for half, (cw_ref, cb_ref, p_ref, s_ref) in enumerate(
            ((cwg_ref, cbg_ref, pg_ref, sg_ref), (cwv_ref, cbv_ref, pv_ref, sv_ref))):
        u = jnp.dot(a, wbf[half], preferred_element_type=F32)
        cw = cw_ref[...]
        if decode:
            p0 = p_ref[:, 0, :]
            p1 = p_ref[:, 1, :]
            c = cb_ref[...] + cw[0:1] * p0 + cw[1:2] * p1 + cw[2:3] * u
            s_ref[:, 0, :] = p1
            s_ref[:, 1, :] = u
        else:
            first = (i % tiles_per_seq) == 0
            up = jnp.dot(ap_ref[...], wbf[half], preferred_element_type=F32)
            prev = p_ref[0]
            m1 = jnp.where(first, prev[1:2], up[7:8])
            m2 = jnp.where(first, prev[0:1], up[6:7])
            r1 = pltpu.roll(u, 1, axis=0)
            r2 = pltpu.roll(u, 2, axis=0)
            u1 = jnp.where(row == 0, m1, r1)
            u2 = jnp.where(row == 0, m2, jnp.where(row == 1, m1, r2))
            c = cb_ref[...] + cw[0:1] * u2 + cw[1:2] * u1 + cw[2:3] * u

            @pl.when((i % tiles_per_seq) == tiles_per_seq - 1)
            def _():
                s_ref[0] = u[tm - 2:tm]
        outs.append(c)
    act_ref[...] = (_silu(outs[0]) * outs[1]).astype(BF16)


def _ffn_up(a, w_up, conv_w, conv_b3, prev, layer, seq_rows, tn=512, tm=512):
    m, d = a.shape
    d_ff = w_up.shape[-1] // 2
    n_seq = m // seq_rows
    decode = seq_rows == 1
    tm = m if decode else min(tm, seq_rows)
    tiles_per_seq = 1 if decode else seq_rows // tm
    nj = d_ff // tn
    pb = tm // 8

    if decode:
        pspec = lambda off: pl.BlockSpec((n_seq, 2, tn), lambda j, i: (0, 0, j + off))
        sspec = pl.BlockSpec((n_seq, 2, tn), lambda j, i: (0, 0, j))
    else:
        pspec = lambda off: pl.BlockSpec((1, 2, tn), lambda j, i: (i // tiles_per_seq, 0, j + off))
        sspec = pl.BlockSpec((1, 2, tn), lambda j, i: (i // tiles_per_seq, 0, j))
    in_specs = [
        pl.BlockSpec((tm, d), lambda j, i: (i, 0)),
        pl.BlockSpec((8, d), lambda j, i: (jnp.maximum(i * pb - 1, 0), 0)),
        pl.BlockSpec((None, d, tn), lambda j, i: (layer, 0, j)),
        pl.BlockSpec((None, d, tn), lambda j, i: (layer, 0, j + nj)),
        pl.BlockSpec((None, CONV_W, tn), lambda j, i: (layer, 0, j)),
        pl.BlockSpec((None, CONV_W, tn), lambda j, i: (layer, 0, j + nj)),
        pl.BlockSpec((None, 1, tn), lambda j, i: (layer, 0, j)),
        pl.BlockSpec((None, 1, tn), lambda j, i: (layer, 0, j + nj)),
        pspec(0), pspec(nj),
    ]
    return pl.pallas_call(
        functools.partial(_ffn_up_kernel, tiles_per_seq=tiles_per_seq, decode=decode),
        out_shape=[jax.ShapeDtypeStruct((m, d_ff), BF16),
                   jax.ShapeDtypeStruct((n_seq, 2, d_ff), F32),
                   jax.ShapeDtypeStruct((n_seq, 2, d_ff), F32)],
        grid=(nj, m // tm),
        in_specs=in_specs,
        out_specs=[pl.BlockSpec((tm, tn), lambda j, i: (i, j)), sspec, sspec],
        scratch_shapes=[pltpu.VMEM((2, d, tn), BF16)],
        compiler_params=_cparams(2),
        name="ffn_up",
    )(a, a, w_up, w_up, conv_w, conv_w, conv_b3, conv_b3, prev, prev)


def _t5_bucket(dist):
    dist = jnp.maximum(dist, 0)
    max_exact = NUM_BUCKETS // 2
    log_ratio = jnp.log(jnp.maximum(dist, 1).astype(F32) / max_exact) / math.log(MAX_DISTANCE / max_exact)
    large = jnp.minimum(max_exact + (log_ratio * (NUM_BUCKETS - max_exact)).astype(I32), NUM_BUCKETS - 1)
    return jnp.where(dist < max_exact, dist, large)


def _t5_lookup(t5_table, dist):
    onehot = (_t5_bucket(dist)[..., None] == jnp.arange(NUM_BUCKETS, dtype=I32)).astype(F32)
    return jnp.einsum("...k,kh->...h", onehot, t5_table.astype(F32), precision=lax.Precision.HIGHEST)


def _t5_near_tiles(t5_table):
    r = jnp.arange(LANE, dtype=I32)
    dist = jnp.arange(2, dtype=I32)[:, None, None] * LANE + r[None, :, None] - r[None, None, :]
    bias = jnp.moveaxis(_t5_lookup(t5_table, dist), -1, 0)
    far = t5_table[NUM_BUCKETS - 1].astype(F32)[:, None, None, None]
    return jnp.where(dist[None] >= 0, bias - far, NEG_INF)


def _fold_lanes(x, op):
    acc = x[:, :LANE]
    for t in range(1, x.shape[1] // LANE):
        acc = op(acc, x[:, t * LANE:(t + 1) * LANE])
    return acc


def _attend(q_bf, k_ref, v_ref, ls_ref, st_ref, i, far_bias, t0, t1):
    def logits(off):
        s = lax.dot_general(q_bf, k_ref[pl.ds(off, CH), :], (((1,), (1,)), ((), ())), preferred_element_type=F32)
        b = far_bias(off)
        return s if b is None else s + b

    st_ref[0] = jnp.full((CH, LANE), -jnp.inf, F32)

    def far_chunk(c, carry):
        off = pl.multiple_of(c * CH, CH)
        s = logits(off)
        ls_ref[:, pl.ds(off, CH)] = s
        st_ref[0] = jnp.maximum(st_ref[0], _fold_lanes(s, jnp.maximum))
        return carry

    lax.fori_loop(0, i, far_chunk, 0)

    if t1 is not None:
        @pl.when(i > 0)
        def _():
            off = pl.multiple_of(i * CH - LANE, LANE)
            sub = ls_ref[0:LANE, pl.ds(off, LANE)] + t1
            ls_ref[0:LANE, pl.ds(off, LANE)] = sub
            st_ref[0, 0:LANE] = jnp.maximum(st_ref[0, 0:LANE], sub)

    off_d = pl.multiple_of(i * CH, CH)
    s = logits(off_d)
    for a in range(SUB):
        rows = slice(a * LANE, (a + 1) * LANE)
        pieces = []
        for b in range(SUB):
            if b > a:
                pieces.append(jnp.full((LANE, LANE), NEG_INF, F32))
                continue
            blk = s[rows, b * LANE:(b + 1) * LANE]
            if b == a:
                blk = blk + t0
            elif b == a - 1 and t1 is not None:
                blk = blk + t1
            pieces.append(blk)
        sa = jnp.concatenate(pieces, axis=1)
        ls_ref[rows, pl.ds(off_d, CH)] = sa
        st_ref[0, rows] = jnp.maximum(st_ref[0, rows], _fold_lanes(sa, jnp.maximum))

    st_ref[0] = jnp.broadcast_to(jnp.max(st_ref[0], axis=1, keepdims=True), (CH, LANE))
    st_ref[1] = jnp.zeros((CH, LANE), F32)
    st_ref[2] = jnp.zeros((CH, LANE), F32)

    def pv_chunk(c, carry):
        off = pl.multiple_of(c * CH, CH)
        chunk = ls_ref[:, pl.ds(off, CH)]
        m = st_ref[0]
        ps = [jnp.exp(chunk[:, t * LANE:(t + 1) * LANE] - m) for t in range(SUB)]
        st_ref[1] = st_ref[1] + functools.reduce(jnp.add, ps)
        p = jnp.concatenate(ps, axis=1).astype(BF16)
        st_ref[2] = st_ref[2] + jnp.dot(p, v_ref[pl.ds(off, CH), :], preferred_element_type=F32)
        return carry

    lax.fori_loop(0, i + 1, pv_chunk, 0)
    return st_ref[2] / jnp.sum(st_ref[1], axis=1, keepdims=True)


def _logf_suffix_kernel(fg_ref, b_ref, lf_ref, r_ref):
    z = fg_ref[...] + b_ref[...]
    lf = jnp.minimum(z, 0.0) - jnp.log(1.0 + jnp.exp(-jnp.abs(z)))
    lf_ref[...] = lf
    n = lf.shape[1]
    lane = lax.broadcasted_iota(I32, lf.shape, 1)
    x = lf
    k = 1
    while k < n:
        x = x + jnp.where(lane >= k, pltpu.roll(x, k, axis=1), 0.0)
        k *= 2
    r_ref[...] = x[:, n - 1:n] - x


def _logf_suffix(fg_t, b_col):
    return pl.pallas_call(
        _logf_suffix_kernel,
        out_shape=[jax.ShapeDtypeStruct(fg_t.shape, F32)] * 2,
        compiler_params=pltpu.CompilerParams(vmem_limit_bytes=VMEM_LIMIT),
        name="logf_suffix",
    )(fg_t, b_col)


def _causal_tile():
    row = lax.broadcasted_iota(I32, (LANE, LANE), 0)
    col = lax.broadcasted_iota(I32, (LANE, LANE), 1)
    return jnp.where(col <= row, 0.0, NEG_INF)


def _fox_kernel(q_ref, k_ref, v_ref, rk_ref, rq_ref, o_ref, ls_ref, st_ref):
    i = pl.program_id(2)
    q_bf = (q_ref[...] * SCALE).astype(BF16)
    rq = rq_ref[...]
    far = lambda off: rk_ref[:, pl.ds(off, CH)] - rq
    o = _attend(q_bf, k_ref, v_ref, ls_ref, st_ref, i, far, _causal_tile(), None)
    o_ref[...] = o.astype(o_ref.dtype)


def _fox_prompt(q, kbf, vbf, rk, rq, bsz, seq):
    nq = seq // CH
    return pl.pallas_call(
        _fox_kernel,
        out_shape=jax.ShapeDtypeStruct(q.shape, BF16),
        grid=(bsz, H_C, nq),
        in_specs=[pl.BlockSpec((CH, HEAD_DIM), lambda b, h, i: (b * nq + i, h)),
                  pl.BlockSpec((seq, HEAD_DIM), lambda b, h, i: (b, h)),
                  pl.BlockSpec((seq, HEAD_DIM), lambda b, h, i: (b, h)),
                  pl.BlockSpec((None, None, 1, seq), lambda b, h, i: (b, h, 0, 0)),
                  pl.BlockSpec((None, None, CH, 1), lambda b, h, i: (b, h, i, 0))],
        out_specs=pl.BlockSpec((CH, HEAD_DIM), lambda b, h, i: (b * nq + i, h)),
        scratch_shapes=[pltpu.VMEM((CH, seq), F32), pltpu.VMEM((3, CH, LANE), F32)],
        compiler_params=_cparams(3),
        name="fox_prompt",
    )(q, kbf, vbf, rk, rq)


def _ordered_key(score):
    score = jnp.where(score == 0.0, 0.0, score)
    bits = pltpu.bitcast(score, I32)
    return jnp.where(bits < 0, bits ^ 0x7FFFFFFF, bits)


def _kth_largest_key(count_ge, k, shape):
    def body(step, r_u):
        cand_u = r_u | lax.shift_left(jnp.int32(1), 31 - step)
        return jnp.where(count_ge(cand_u ^ INT_MIN) >= k, cand_u, r_u)

    return lax.fori_loop(0, 32, body, jnp.zeros(shape, I32)) ^ INT_MIN


def _dsa_kernel(q_ref, tail_ref, kidx_ref, k_ref, v_ref, tn_ref, o_ref, work_ref, mb_ref, st_ref, *, k_sel):
    i = pl.program_id(1)
    tail = tail_ref[...]
    iw = H_IDX * D_IDX + D_IDX
    qi = jnp.concatenate([tail[:, h * D_IDX:(h + 1) * D_IDX] for h in range(H_IDX)], axis=0).astype(BF16)
    wi = [tail[:, iw + h:iw + h + 1] for h in range(H_IDX)]
    row = lax.broadcasted_iota(I32, (CH, CH), 0)
    col = lax.broadcasted_iota(I32, (CH, CH), 1)

    def keys_at(rows, off, width):
        return pltpu.bitcast(work_ref[rows, pl.ds(off, width)], I32)

    def score_chunk(off, diagonal):
        kt = kidx_ref[pl.ds(off, CH), :].astype(BF16)
        s = lax.dot_general(qi, kt, (((1,), (1,)), ((), ())), preferred_element_type=F32)
        score = jnp.zeros((CH, CH), F32)
        for h in range(H_IDX):
            score = score + jnp.maximum(s[h * CH:(h + 1) * CH], 0.0) * wi[h]
        if diagonal:
            score = jnp.where(col <= row, score, NEG_INF)
        work_ref[:, pl.ds(off, CH)] = pltpu.bitcast(_ordered_key(score), F32)

    def far_scores(c, carry):
        score_chunk(pl.multiple_of(c * CH, CH), False)
        return carry

    lax.fori_loop(0, i, far_scores, 0)
    off_d = pl.multiple_of(i * CH, CH)
    score_chunk(off_d, True)

    lrow = lax.broadcasted_iota(I32, (LANE, LANE), 0)
    lcol = lax.broadcasted_iota(I32, (LANE, LANE), 1)
    incl = jnp.where(lrow <= lcol, 1.0, 0.0).astype(BF16)

    for a in range(SUB):
        rows = slice(a * LANE, (a + 1) * LANE)

        def count(pred, rows=rows, a=a):
            def body(c, acc):
                blk = keys_at(rows, pl.multiple_of(c * CH, CH), CH)
                for t in range(SUB):
                    acc = acc + jnp.where(pred(blk[:, t * LANE:(t + 1) * LANE]), 1, 0)
                return acc
            acc = lax.fori_loop(0, i, body, jnp.zeros((LANE, LANE), I32))
            blk = keys_at(rows, off_d, (a + 1) * LANE)
            for t in range(a + 1):
                acc = acc + jnp.where(pred(blk[:, t * LANE:(t + 1) * LANE]), 1, 0)
            return jnp.sum(acc, axis=1, keepdims=True)

        thr = _kth_largest_key(lambda t: count(lambda key: key >= t), k_sel, (LANE, 1))
        n_ge = count(lambda key: key >= thr)
        need = (k_sel - count(lambda key: key > thr)).astype(F32)
        all_ties_kept = jnp.max(n_ge.astype(F32)) <= k_sel

        @pl.when(all_ties_kept)
        def _(rows=rows, thr=thr):
            def body(c, carry):
                off = pl.multiple_of(c * CH, CH)
                mb_ref[rows, pl.ds(off, CH)] = jnp.where(keys_at(rows, off, CH) >= thr, 0.0, NEG_INF)
                return carry
            lax.fori_loop(0, i + 1, body, 0)

        @pl.when(jnp.logical_not(all_ties_kept))
        def _(rows=rows, thr=thr, need=need):
            def body(c, seen):
                off = pl.multiple_of(c * LANE, LANE)
                key = keys_at(rows, off, LANE)
                eq = key == thr
                pre = jnp.dot(jnp.where(eq, 1.0, 0.0).astype(BF16), incl, preferred_element_type=F32)
                sel = (key > thr) | (eq & (pre + seen <= need))
                mb_ref[rows, pl.ds(off, LANE)] = jnp.where(sel, 0.0, NEG_INF)
                return seen + pre[:, LANE - 1:LANE]
            lax.fori_loop(0, (i + 1) * SUB, body, jnp.zeros((LANE, 1), F32))

    for h in range(H_A):
        hs = slice(h * HEAD_DIM, (h + 1) * HEAD_DIM)
        q_bf = (q_ref[:, hs] * SCALE).astype(BF16)
        far = lambda off: mb_ref[:, pl.ds(off, CH)]
        o = _attend(q_bf, k_ref.at[:, hs], v_ref.at[:, hs], work_ref, st_ref, i, far, tn_ref[h, 0], tn_ref[h, 1])
        o_ref[:, hs] = o.astype(o_ref.dtype)


def _dsa_prompt(q, tail, kidx, kbf, vbf, tn, bsz, seq):
    nq = seq // CH
    wa = H_A * HEAD_DIM
    k_sel = min(DSA_TOPK, seq // 4)
    once = dict(pipeline_mode=pl.Buffered(1))
    return pl.pallas_call(
        functools.partial(_dsa_kernel, k_sel=k_sel),
        out_shape=jax.ShapeDtypeStruct(q.shape, BF16),
        grid=(bsz, nq),
        in_specs=[pl.BlockSpec((CH, wa), lambda b, i: (b * nq + i, 0)),
                  pl.BlockSpec((CH, tail.shape[1]), lambda b, i: (b * nq + i, 0)),
                  pl.BlockSpec((seq, D_IDX), lambda b, i: (b, 0), **once),
                  pl.BlockSpec((seq, wa), lambda b, i: (b, 0), **once),
                  pl.BlockSpec((seq, wa), lambda b, i: (b, 0), **once),
                  pl.BlockSpec((H_A, 2, LANE, LANE), lambda b, i: (0, 0, 0, 0), **once)],
        out_specs=pl.BlockSpec((CH, wa), lambda b, i: (b * nq + i, 0)),
        scratch_shapes=[pltpu.VMEM((CH, seq), F32), pltpu.VMEM((CH, seq), F32), pltpu.VMEM((3, CH, LANE), F32)],
        compiler_params=_cparams(2),
        name="dsa_prompt",
    )(q, tail, kidx, kbf, vbf, tn)


def _top_blocks(gate, n_valid, n_top):
    nblk = gate.shape[1]
    blk = lax.broadcasted_iota(I32, gate.shape, 1)
    g = jnp.where(blk < n_valid, gate, NEG_INF)
    rank = jnp.zeros(gate.shape, I32)
    for m in range(nblk):
        gm = g[:, m:m + 1]
        rank = rank + jnp.where((gm > g) | ((gm == g) & (m < blk)), 1, 0)
    return jnp.where((blk < n_valid) & (rank < n_top), 1.0, 0.0)


def _moba_kernel(o_in_ref, q_ref, km_ref, k_ref, v_ref, tn_ref, o_ref, ls_ref, st_ref, *, n_top):
    del o_in_ref
    i = pl.program_id(2)
    q = q_ref[...]
    gate = lax.dot_general(q, km_ref[...], (((1,), (1,)), ((), ())), preferred_element_type=F32,
                           precision=lax.Precision.HIGHEST)
    own = (i * CH + lax.broadcasted_iota(I32, (CH, 1), 0)) // MOBA_BLOCK
    blk = lax.broadcasted_iota(I32, gate.shape, 1)
    sel = jnp.where(blk == own, 1.0, _top_blocks(gate, own, n_top))
    q_bf = (q * SCALE).astype(BF16)
    per_ch = CH // MOBA_BLOCK

    def far(off):
        first = off // MOBA_BLOCK
        cols = []
        for t in range(per_ch):
            picked = jnp.sum(jnp.where(blk == first + t, sel, 0.0), axis=1, keepdims=True)
            cols.append(jnp.broadcast_to(jnp.where(picked > 0.5, 0.0, NEG_INF), (CH, MOBA_BLOCK)))
        return jnp.concatenate(cols, axis=1)

    o = _attend(q_bf, k_ref, v_ref, ls_ref, st_ref, i, far, tn_ref[0, 0], tn_ref[0, 1])
    o_ref[...] = o.astype(o_ref.dtype)


def _moba_prompt(o_ab, q, kmean, kbf, vbf, tn, bsz, seq):
    nq = seq // CH
    nblk = kmean.shape[1]
    n_top = min(MOBA_TOPK, nblk)
    return pl.pallas_call(
        functools.partial(_moba_kernel, n_top=n_top),
        out_shape=jax.ShapeDtypeStruct(o_ab.shape, o_ab.dtype),
        grid=(bsz, H_B, nq),
        in_specs=[pl.BlockSpec(memory_space=pl.ANY),
                  pl.BlockSpec((CH, HEAD_DIM), lambda b, h, i: (b * nq + i, H_A + h)),
                  pl.BlockSpec((None, nblk, HEAD_DIM), lambda b, h, i: (b, 0, H_A + h)),
                  pl.BlockSpec((seq, HEAD_DIM), lambda b, h, i: (b, H_A + h)),
                  pl.BlockSpec((seq, HEAD_DIM), lambda b, h, i: (b, H_A + h)),
                  pl.BlockSpec((1, 2, LANE, LANE), lambda b, h, i: (H_A + h, 0, 0, 0))],
        out_specs=pl.BlockSpec((CH, HEAD_DIM), lambda b, h, i: (b * nq + i, H_A + h)),
        scratch_shapes=[pltpu.VMEM((CH, seq), F32), pltpu.VMEM((3, CH, LANE), F32)],
        input_output_aliases={0: 0},
        compiler_params=_cparams(3),
        name="moba_prompt",
    )(o_ab, q, kmean, kbf, vbf, tn)


def _paged_specs(n, block, layer, pages_of):
    def one(r):
        zeros = (0,) * (len(block) - 2)
        return pl.BlockSpec(block, lambda b, g, pt: (layer, pt[b, pages_of(b, g, r)]) + zeros)
    return [one(r) for r in range(n)]


def _idx_scores_kernel(pt_ref, *refs, n_pg):
    del pt_ref
    kp, (qi_ref, wi_ref, o_ref) = refs[:n_pg], refs[n_pg:]
    qi = qi_ref[...].astype(BF16)
    wi = wi_ref[...]
    for r in range(n_pg):
        s = lax.dot_general(qi, kp[r][...].astype(BF16), (((1,), (1,)), ((), ())), preferred_element_type=F32)
        o_ref[r:r + 1, :] = jnp.sum(jnp.maximum(s, 0.0) * wi, axis=0, keepdims=True)


def _idx_scores_sample(cache_kidx, li, page_table, qi8, wi8):
    nb, n_pages = page_table.shape
    page = cache_kidx.shape[2]
    n_pg = min(16, n_pages)
    blk = (None, None, page, D_IDX)
    return pl.pallas_call(
        functools.partial(_idx_scores_kernel, n_pg=n_pg),
        out_shape=jax.ShapeDtypeStruct((nb, n_pages, page), F32),
        grid_spec=pltpu.PrefetchScalarGridSpec(
            num_scalar_prefetch=1,
            grid=(nb, n_pages // n_pg),
            in_specs=_paged_specs(n_pg, blk, li, lambda b, g, r: g * n_pg + r)
            + [pl.BlockSpec((None, 8, D_IDX), lambda b, g, pt: (b, 0, 0)),
               pl.BlockSpec((None, 8, 1), lambda b, g, pt: (b, 0, 0))],
            out_specs=pl.BlockSpec((None, n_pg, page), lambda b, g, pt: (b, g, 0))),
        compiler_params=_cparams(2),
        name="idx_scores_sample",
    )(page_table, *([cache_kidx] * n_pg), qi8, wi8)


def _dsa_select_kernel(sc_ref, qi_ref, wi_ref, kin_ref, mb_ref, mbn_ref, *, k_sel):
    nb, n_pages, page = sc_ref.shape
    incl = jnp.where(lax.broadcasted_iota(I32, (page, page), 0) <= lax.broadcasted_iota(I32, (page, page), 1),
                     1.0, 0.0).astype(BF16)
    before = jnp.where(lax.broadcasted_iota(I32, (n_pages, n_pages), 1) < lax.broadcasted_iota(I32, (n_pages, n_pages), 0),
                       1.0, 0.0).astype(BF16)

    def total(x):
        return jnp.sum(jnp.sum(x, axis=0, keepdims=True), axis=1, keepdims=True)

    for b in range(nb):
        s_new = jnp.sum(qi_ref[b] * kin_ref[b:b + 1, :], axis=1, keepdims=True)
        s_new = jnp.sum(jnp.maximum(s_new, 0.0) * wi_ref[b], axis=0, keepdims=True)
        key = _ordered_key(sc_ref[b])
        key_new = _ordered_key(jnp.broadcast_to(s_new, (8, LANE)))[0:1, 0:1]

        def count_ge(t):
            return total(jnp.where(key >= t, 1, 0)) + jnp.where(key_new >= t, 1, 0)

        thr = _kth_largest_key(count_ge, k_sel, (1, 1))
        gt = key > thr
        need = (k_sel - total(jnp.where(gt, 1, 0)) - jnp.where(key_new > thr, 1, 0)).astype(F32)
        eq = key == thr
        eq_bf = jnp.where(eq, 1.0, 0.0).astype(BF16)
        pre = jnp.dot(eq_bf, incl, preferred_element_type=F32)
        off = jnp.sum(jnp.dot(before, eq_bf, preferred_element_type=F32), axis=1, keepdims=True)
        sel = gt | (eq & (pre + off <= need))
        mb_ref[b] = jnp.where(sel, 0.0, NEG_INF)
        n_eq = total(jnp.where(eq, 1.0, 0.0))
        sel_new = (key_new > thr) | ((key_new == thr) & (n_eq + 1.0 <= need))
        mbn_ref[b:b + 1, :] = jnp.broadcast_to(jnp.where(sel_new, 0.0, NEG_INF), (1, LANE))


def _dsa_select_sample(scores, qi8, wi8, ki_new):
    nb, n_pages, page = scores.shape
    k_sel = min(DSA_TOPK, (n_pages * page + 1) // 4)
    return pl.pallas_call(
        functools.partial(_dsa_select_kernel, k_sel=k_sel),
        out_shape=[jax.ShapeDtypeStruct(scores.shape, F32), jax.ShapeDtypeStruct((nb, LANE), F32)],
        compiler_params=pltpu.CompilerParams(vmem_limit_bytes=VMEM_LIMIT),
        name="dsa_select_sample",
    )(scores, qi8, wi8, ki_new)


def _logf_suffix_sample_kernel(pt_ref, *refs, n_pg):
    del pt_ref
    lp, (fg_ref, bf_ref, r_ref, lfn_ref, carry) = refs[:n_pg], refs[n_pg:]
    page = lp[0].shape[0]

    @pl.when(pl.program_id(1) == 0)
    def _():
        z = fg_ref[...] + bf_ref[...]
        lfn = jnp.minimum(z, 0.0) - jnp.log(1.0 + jnp.exp(-jnp.abs(z)))
        lfn_ref[...] = lfn
        carry[...] = lfn

    row = lax.broadcasted_iota(I32, (page, H_C), 0)
    c = carry[...]
    for r in reversed(range(n_pg)):
        lf = lp[r][...]
        x = lf
        k = 1
        while k < page:
            x = x + jnp.where(row + k < page, pltpu.roll(x, page - k, axis=0), 0.0)
            k *= 2
        r_ref[r * page:(r + 1) * page, :] = x - lf + c
        c = c + x[0:1, :]
    carry[...] = c


def _logf_suffix_sample(cache_logf, li, page_table, fg_new, b_f):
    nb, n_pages = page_table.shape
    page = cache_logf.shape[2]
    n_pg = min(16, n_pages)
    n_steps = n_pages // n_pg
    blk = (None, None, page, H_C)
    return pl.pallas_call(
        functools.partial(_logf_suffix_sample_kernel, n_pg=n_pg),
        out_shape=[jax.ShapeDtypeStruct((nb, n_pages * page, H_C), F32), jax.ShapeDtypeStruct((nb, 1, H_C), F32)],
        grid_spec=pltpu.PrefetchScalarGridSpec(
            num_scalar_prefetch=1,
            grid=(nb, n_steps),
            in_specs=_paged_specs(n_pg, blk, li, lambda b, g, r: (n_steps - 1 - g) * n_pg + r)
            + [pl.BlockSpec((None, 1, H_C), lambda b, g, pt: (b, 0, 0)),
               pl.BlockSpec((1, H_C), lambda b, g, pt: (0, 0))],
            out_specs=[pl.BlockSpec((None, n_pg * page, H_C), lambda b, g, pt: (b, n_steps - 1 - g, 0)),
                       pl.BlockSpec((None, 1, H_C), lambda b, g, pt: (b, 0, 0))],
            scratch_shapes=[pltpu.VMEM((1, H_C), F32)]),
        compiler_params=_cparams(2),
        name="logf_suffix_sample",
    )(page_table, *([cache_logf] * n_pg), fg_new, b_f)


def _decode_partials_kernel(pt_ref, *refs, n_pg, has_mask):
    del pt_ref
    kp, vp, rest = refs[:n_pg], refs[n_pg:2 * n_pg], refs[2 * n_pg:]
    if has_mask:
        qbd_ref, ebd_ref, bias_ref, mask_ref, m_ref, l_ref, acc_ref, ks_ref, kbf = rest
    else:
        qbd_ref, ebd_ref, bias_ref, m_ref, l_ref, acc_ref, ks_ref, kbf = rest
    page = kp[0].shape[0] // H_AB
    heads = [slice(h * HEAD_DIM, (h + 1) * HEAD_DIM) for h in range(H_AB)]
    for r in range(n_pg):
        for h in range(H_AB):
            kbf[r * page:(r + 1) * page, heads[h]] = kp[r][pl.ds(h, page, stride=H_AB), :].astype(BF16)
    logits = jnp.dot(kbf[...], qbd_ref[...], preferred_element_type=F32)
    s = logits[:, :H_AB] + bias_ref[...]
    if has_mask:
        eye = lax.broadcasted_iota(I32, (page, page), 0) == lax.broadcasted_iota(I32, (page, page), 1)
        first = pl.program_id(1) * n_pg
        cols = [jnp.sum(jnp.where(eye, mask_ref[pl.ds(first + r, 1), :], 0.0), axis=1, keepdims=True)
                for r in range(n_pg)]
        lane = lax.broadcasted_iota(I32, s.shape, 1)
        s = s + jnp.where(lane < H_A, jnp.concatenate(cols, axis=0), 0.0)
    per_blk = MOBA_BLOCK // page
    for blk in range(n_pg // per_blk):
        rows = slice(blk * MOBA_BLOCK, (blk + 1) * MOBA_BLOCK)
        sb = s[rows]
        m = jnp.max(sb, axis=0, keepdims=True)
        p = jnp.exp(sb - m)
        m_ref[blk:blk + 1, :] = m
        l_ref[blk:blk + 1, :] = jnp.sum(p, axis=0, keepdims=True)
        pexp = jnp.dot(p.astype(BF16), ebd_ref[...], preferred_element_type=F32)
        pages = range(blk * per_blk, (blk + 1) * per_blk)
        for h in range(H_AB):
            acc_ref[blk:blk + 1, heads[h]] = sum(
                jnp.sum(pexp[(pg - pages[0]) * page:(pg - pages[0] + 1) * page, heads[h]]
                        * vp[pg][pl.ds(h, page, stride=H_AB), :], axis=0, keepdims=True) for pg in pages)
        ks_ref[blk] = sum(jnp.sum(kp[pg][...].reshape(page, H_AB, HEAD_DIM), axis=0) for pg in pages)


def _decode_partials(cache_k, cache_v, li, page_table, qbd, ebd, bias, mask):
    nb, n_pages = page_table.shape
    page = cache_k.shape[2] // H_AB
    n_pg = 2 * (MOBA_BLOCK // page)
    n_steps = n_pages // n_pg
    nblk = n_pg * page // MOBA_BLOCK
    rows = n_pg * page
    blk = (None, None, page * H_AB, HEAD_DIM)
    pages_of = lambda b, g, r: g * n_pg + r
    per_b = bias.shape[0] > 1
    in_specs = (_paged_specs(n_pg, blk, li, pages_of) + _paged_specs(n_pg, blk, li, pages_of)
                + [pl.BlockSpec((None, D_HEADS, LANE), lambda b, g, pt: (b, 0, 0)),
                   pl.BlockSpec((H_AB, D_HEADS), lambda b, g, pt: (0, 0)),
                   pl.BlockSpec((None, rows, H_AB), lambda b, g, pt: (b if per_b else 0, g, 0))])
    args = [page_table] + [cache_k] * n_pg + [cache_v] * n_pg + [qbd, ebd, bias]
    if mask is not None:
        in_specs.append(pl.BlockSpec((None, n_pages, page), lambda b, g, pt: (b, 0, 0)))
        args.append(mask)
    small = pl.BlockSpec((None, None, nblk, H_AB), lambda b, g, pt: (b, g, 0, 0))
    wide = pl.BlockSpec((None, None, nblk, D_HEADS), lambda b, g, pt: (b, g, 0, 0))
    wide3 = pl.BlockSpec((None, None, nblk, H_AB, HEAD_DIM), lambda b, g, pt: (b, g, 0, 0, 0))
    m, l, acc, ks = pl.pallas_call(
        functools.partial(_decode_partials_kernel, n_pg=n_pg, has_mask=mask is not None),
        out_shape=[jax.ShapeDtypeStruct((nb, n_steps, nblk, H_AB), F32)] * 2
        + [jax.ShapeDtypeStruct((nb, n_steps, nblk, D_HEADS), F32),
           jax.ShapeDtypeStruct((nb, n_steps, nblk, H_AB, HEAD_DIM), F32)],
        grid_spec=pltpu.PrefetchScalarGridSpec(
            num_scalar_prefetch=1,
            grid=(nb, n_steps),
            in_specs=in_specs,
            out_specs=[small, small, wide, wide3],
            scratch_shapes=[pltpu.VMEM((rows, D_HEADS), BF16)]),
        compiler_params=_cparams(2),
        name="decode_partials",
    )(*args)
    nb_all = n_steps * nblk
    return (m.reshape(nb, nb_all, H_AB), l.reshape(nb, nb_all, H_AB),
            acc.reshape(nb, nb_all, D_HEADS), ks.reshape(nb, nb_all, H_AB, HEAD_DIM))


def _decode_combine_kernel(m_ref, l_ref, acc_ref, ks_ref, q_ref, kn_ref, vn_ref, bn_ref, o_ref, *, moba_from, n_top):
    q = q_ref[...]
    prod = q * kn_ref[...]
    nblk = m_ref.shape[0]
    blk = lax.broadcasted_iota(I32, (nblk, 1), 0)
    for h in range(H_AB):
        hs = slice(h * HEAD_DIM, (h + 1) * HEAD_DIM)
        lg_new = jnp.sum(prod[:, hs], axis=1, keepdims=True) * SCALE + bn_ref[:, h:h + 1]
        m_h = m_ref[:, h:h + 1]
        if h >= moba_from:
            gate = jnp.sum(ks_ref[:, h, :] * (1.0 / MOBA_BLOCK) * q[:, hs], axis=1, keepdims=True)
            keep = jnp.zeros((nblk, 1), jnp.bool_)
            for _ in range(n_top):
                best = jnp.max(gate, axis=0, keepdims=True)
                first = jnp.min(jnp.where(gate == best, blk, nblk), axis=0, keepdims=True)
                keep = keep | (blk == first)
                gate = jnp.where(blk == first, -jnp.inf, gate)
            m_h = jnp.where(keep, m_h, -jnp.inf)
        top = jnp.maximum(jnp.max(m_h, axis=0, keepdims=True), lg_new)
        w = jnp.exp(m_h - top)
        e_new = jnp.exp(lg_new - top)
        den = jnp.sum(w * l_ref[:, h:h + 1], axis=0, keepdims=True) + e_new
        num = jnp.sum(w * acc_ref[:, hs], axis=0, keepdims=True) + e_new * vn_ref[:, hs]
        o_ref[:, hs] = (num / den).astype(o_ref.dtype)


def _decode_combine(m, l, acc, ks, q, k_new, v_new, bias_new, moba_from):
    nb, nblk, _ = m.shape
    n_top = min(MOBA_TOPK, nblk)
    small = pl.BlockSpec((None, nblk, H_AB), lambda b: (b, 0, 0))
    wide = pl.BlockSpec((None, nblk, D_HEADS), lambda b: (b, 0, 0))
    row = pl.BlockSpec((None, 1, D_HEADS), lambda b: (b, 0, 0))
    out = pl.pallas_call(
        functools.partial(_decode_combine_kernel, moba_from=moba_from, n_top=n_top),
        out_shape=jax.ShapeDtypeStruct((nb, 1, D_HEADS), BF16),
        grid=(nb,),
        in_specs=[small, small, wide, pl.BlockSpec((None, nblk, H_AB, HEAD_DIM), lambda b: (b, 0, 0, 0)),
                  row, row, row, pl.BlockSpec((None, 1, LANE), lambda b: (b, 0, 0))],
        out_specs=row,
        compiler_params=_cparams(1),
        name="decode_combine",
    )(m, l, acc, ks, q.reshape(nb, 1, D_HEADS), k_new.reshape(nb, 1, D_HEADS), v_new.reshape(nb, 1, D_HEADS),
      bias_new.reshape(nb, 1, LANE))
    return out.reshape(nb, D_HEADS)


def _ab_cols(tn):
    per = H_A * HEAD_DIM // tn
    def cols(j):
        base = (j // per) * 3 * per + j % per
        return base, base + per, base + 2 * per
    return cols


def _c_cols(tn):
    per = D_HEADS // tn
    return lambda j: (j, j + per, j + 2 * per)


def _block_diag_queries(q):
    head_of_row = jnp.arange(D_HEADS, dtype=I32) // HEAD_DIM
    onehot = (head_of_row[:, None] == jnp.arange(LANE, dtype=I32)[None, :]).astype(F32)
    return ((q * SCALE)[:, :, None] * onehot[None]).astype(BF16)


def kernel(x_prompt, x_sample, cache_k_ab, cache_v_ab, cache_kidx, cache_k_c, cache_v_c, cache_logf_c, state_conv,
           page_table, g_mix, g_ffn, g_final, w_in_ab, w_out_ab, t5_table, w_in_c, b_forget, w_out_c, w_up, conv_w,
           conv_b, w_down):
    bsz, seq, d = x_prompt.shape
    nb, dseq, _ = x_sample.shape
    depth = g_mix.shape[0]
    d_ff = w_down.shape[1]
    n_pages = page_table.shape[1]
    page = cache_k_ab.shape[2]
    past = n_pages * page
    assert dseq == 1 and d == D_HEADS and seq % CH == 0 and past % MOBA_BLOCK == 0
    t_p = bsz * seq
    tn_ffn = 512 if d_ff % 512 == 0 else LANE

    qkv_w = 3 * D_HEADS
    n_tail_ab = w_in_ab.shape[-1] - qkv_w
    n_tail_c = w_in_c.shape[-1] - qkv_w
    w_tail_ab = jnp.pad(w_in_ab[:, :, qkv_w:], ((0, 0), (0, 0), (0, TAIL_PAD - n_tail_ab)))
    w_tail_c = jnp.pad(w_in_c[:, :, qkv_w:], ((0, 0), (0, 0), (0, LANE - n_tail_c)))
    conv_b3 = conv_b.reshape(depth, 1, 2 * d_ff)
    ck_ab, cv_ab, ck_c, cv_c = (c.reshape(c.shape[0], c.shape[1], page * H_AB, HEAD_DIM)
                                for c in (cache_k_ab, cache_v_ab, cache_k_c, cache_v_c))

    tn = _t5_near_tiles(t5_table)
    t5_past = _t5_lookup(t5_table, past - jnp.arange(past, dtype=I32))[None]
    t5_self = jnp.pad(_t5_lookup(t5_table, jnp.zeros((1,), I32)), ((0, 0), (0, LANE - H_AB)))
    lane = jnp.arange(LANE, dtype=I32)[None]
    ebd = (jnp.arange(H_AB, dtype=I32)[:, None] == (jnp.arange(D_HEADS, dtype=I32) // HEAD_DIM)[None, :]).astype(BF16)
    zero_conv = jnp.zeros((bsz, CONV_W - 1, 2 * d_ff), F32)
    iq, ik, iw = H_IDX * D_IDX, H_IDX * D_IDX + D_IDX, H_IDX * D_IDX + D_IDX + H_IDX

    xp = x_prompt.reshape(t_p, d)
    xs = x_sample.reshape(nb, d)
    outs = {name: [] for name in ("kidx_p", "lf_p", "cv_p", "kidx_s", "lf_s", "cv_s")}
    n_ab, n_c = (depth + 1) // 2, depth // 2
    kv_ab_p = kv_ab_s = kv_c_p = kv_c_s = None
    for layer in range(depth):
        li = layer // 2
        hp = _rmsnorm(xp, g_mix[layer], BF16)
        hs = _rmsnorm(xs, g_mix[layer], BF16)
        if layer % 2 == 0:
            q, k_st, v_st, kbf, vbf, kmean = _qkv_proj(hp, w_in_ab, li, _ab_cols(256), True, n_ab, kv_ab_p)
            kv_ab_p = (k_st, v_st)
            tail = _matmul(hp, w_tail_ab, li, TAIL_PAD)
            kidx = tail[:, iq:ik]
            o = _dsa_prompt(q, tail, kidx, kbf, vbf, tn, bsz, seq)
            o = _moba_prompt(o, q, kmean.reshape(bsz, seq // MOBA_BLOCK, D_HEADS), kbf, vbf, tn, bsz, seq)
            xp = _matmul(o, w_out_ab, li, 512, res=xp)
            outs["kidx_p"].append(kidx.reshape(bsz, seq, D_IDX))
            q, k_st, v_st, _, _ = _qkv_proj(hs, w_in_ab, li, _ab_cols(256), False, n_ab, kv_ab_s)
            kv_ab_s = (k_st, v_st)
            k, v = k_st[li], v_st[li]
            tail = _matmul(hs, w_tail_ab, li, TAIL_PAD)
            qi8 = jnp.pad(tail[:, :iq].reshape(nb, H_IDX, D_IDX), ((0, 0), (0, 8 - H_IDX), (0, 0)))
            wi8 = jnp.pad(tail[:, ik:iw].reshape(nb, H_IDX, 1), ((0, 0), (0, 8 - H_IDX), (0, 0)))
            ki_new = tail[:, iq:ik]
            scores = _idx_scores_sample(cache_kidx, li, page_table, qi8, wi8)
            mb, mb_new = _dsa_select_sample(scores, qi8, wi8, ki_new)
            parts = _decode_partials(ck_ab, cv_ab, li, page_table, _block_diag_queries(q), ebd, t5_past,
                                     mb)
            bias_new = t5_self + jnp.where(lane < H_A, mb_new, 0.0)
            o = _decode_combine(*parts, q, k, v, bias_new, H_A)
            xs = _matmul(o, w_out_ab, li, 512, res=xs)
            outs["kidx_s"].append(ki_new.reshape(nb, 1, D_IDX))
        else:
            q, k_st, v_st, kbf, vbf = _qkv_proj(hp, w_in_c, li, _c_cols(256), False, n_c, kv_c_p)
            kv_c_p = (k_st, v_st)
            tail = _matmul(hp, w_tail_c, li, LANE)
            fg_t = jnp.transpose(tail[:, :H_C].reshape(bsz, seq, H_C), (0, 2, 1)).reshape(bsz * H_C, seq)
            lf_t, r_t = _logf_suffix(fg_t, jnp.tile(b_forget[li], bsz).reshape(bsz * H_C, 1))
            o = _fox_prompt(q, kbf, vbf, r_t.reshape(bsz, H_C, 1, seq), r_t.reshape(bsz, H_C, seq, 1), bsz, seq)
            xp = _matmul(o, w_out_c, li, 512, res=xp)
            outs["lf_p"].append(jnp.transpose(lf_t.reshape(bsz, H_C, seq), (0, 2, 1)))
            q, k_st, v_st, _, _ = _qkv_proj(hs, w_in_c, li, _c_cols(256), False, n_c, kv_c_s)
            kv_c_s = (k_st, v_st)
            k, v = k_st[li], v_st[li]
            tail = _matmul(hs, w_tail_c, li, LANE)
            r_past, lf_new = _logf_suffix_sample(cache_logf_c, li, page_table, tail[:, :H_C].reshape(nb, 1, H_C),
                                                 b_forget[li].reshape(1, H_C))
            parts = _decode_partials(ck_c, cv_c, li, page_table, _block_diag_queries(q), ebd, r_past, None)
            o = _decode_combine(*parts, q, k, v, jnp.zeros((nb, LANE), F32), H_C)
            xs = _matmul(o, w_out_c, li, 512, res=xs)
            outs["lf_s"].append(lf_new)
        act, sg, sv = _ffn_up(_rmsnorm(xp, g_ffn[layer], BF16), w_up, conv_w, conv_b3, zero_conv, layer, seq, tn=tn_ffn)
        xp = _matmul(act, w_down, layer, 512, res=xp)
        outs["cv_p"].append(jnp.concatenate([sg, sv], axis=-1))
        act, sg, sv = _ffn_up(_rmsnorm(xs, g_ffn[layer], BF16), w_up, conv_w, conv_b3, state_conv[layer], layer, 1,
                              tn=tn_ffn)
        xs = _matmul(act, w_down, layer, 512, res=xs)
        outs["cv_s"].append(jnp.concatenate([sg, sv], axis=-1))
    y_prompt = _rmsnorm(xp, g_final, F32).reshape(bsz, seq, d)
    y_sample = _rmsnorm(xs, g_final, F32).reshape(nb, 1, d)
    st = lambda name: jnp.stack(outs[name])
    heads_p = lambda a: a.reshape(a.shape[0], bsz, seq, H_AB, HEAD_DIM)
    heads_s = lambda a: a.reshape(a.shape[0], nb, 1, H_AB, HEAD_DIM)
    return (y_prompt, y_sample, heads_p(kv_ab_p[0]), heads_p(kv_ab_p[1]), st("kidx_p"), heads_p(kv_c_p[0]),
            heads_p(kv_c_p[1]), st("lf_p"), st("cv_p"), heads_s(kv_ab_s[0]), heads_s(kv_ab_s[1]), st("kidx_s"),
            heads_s(kv_c_s[0]), heads_s(kv_c_s[1]), st("lf_s"), st("cv_s"))
```

```python
import functools
import math

import jax
import jax.numpy as jnp
import numpy as np
from jax import lax
from jax.experimental import pallas as pl
from jax.experimental.pallas import tpu as pltpu

F32 = jnp.float32
BF16 = jnp.bfloat16
I32 = jnp.int32

HEAD_DIM = 128
H_A = 8
H_B = 8
H_AB = H_A + H_B
H_C = 16
H_IDX = 4
D_IDX = 64
DSA_TOPK = 256
MOBA_BLOCK = 256
MOBA_TOPK = 3
NUM_BUCKETS = 32
MAX_DISTANCE = 128
CONV_W = 3
RMS_EPS = 1e-6
NEG_INF = -1e30
SCALE = HEAD_DIM ** -0.5
D_HEADS = H_AB * HEAD_DIM
TAIL_PAD = 384
LANE = 128
CH = 512
SUB = CH // LANE
INT_MIN = -(2 ** 31)
VMEM_LIMIT = 56 * 1024 * 1024


def _cparams(n_axes):
    return pltpu.CompilerParams(dimension_semantics=("arbitrary",) * n_axes,
                                vmem_limit_bytes=VMEM_LIMIT)


def _rmsnorm_kernel(x_ref, g_ref, o_ref):
    x = x_ref[...]
    y = x * lax.rsqrt(jnp.mean(x * x, axis=-1, keepdims=True) + RMS_EPS)
    o_ref[...] = (y * g_ref[...]).astype(o_ref.dtype)


def _rmsnorm(x, g, out_dtype):
    m, d = x.shape
    tm = min(m, 512)
    return pl.pallas_call(
        _rmsnorm_kernel,
        out_shape=jax.ShapeDtypeStruct((m, d), out_dtype),
        grid=(m // tm,),
        in_specs=[pl.BlockSpec((tm, d), lambda i: (i, 0)),
                  pl.BlockSpec((1, d), lambda i: (0, 0))],
        out_specs=pl.BlockSpec((tm, d), lambda i: (i, 0)),
        compiler_params=_cparams(1),
        name="rmsnorm",
    )(x, g.reshape(1, d))


def _mm_kernel(*refs, has_res):
    if has_res:
        a_ref, w_ref, r_ref, o_ref, wbf = refs
    else:
        a_ref, w_ref, o_ref, wbf = refs

    @pl.when(pl.program_id(1) == 0)
    def _():
        wbf[...] = w_ref[...].astype(BF16)

    acc = jnp.dot(a_ref[...], wbf[...], preferred_element_type=F32)
    if has_res:
        acc = r_ref[...] + acc
    o_ref[...] = acc


def _matmul(a, w, layer, tn, res=None):
    m, k = a.shape
    n = w.shape[-1]
    tm = min(m, 1024 if k <= D_HEADS else 512)
    in_specs = [pl.BlockSpec((tm, k), lambda j, i: (i, 0)),
                pl.BlockSpec((None, k, tn), lambda j, i: (layer, 0, j))]
    args = [a, w]
    if res is not None:
        in_specs.append(pl.BlockSpec((tm, tn), lambda j, i: (i, j)))
        args.append(res)
    return pl.pallas_call(
        functools.partial(_mm_kernel, has_res=res is not None),
        out_shape=jax.ShapeDtypeStruct((m, n), F32),
        grid=(n // tn, m // tm),
        in_specs=in_specs,
        out_specs=pl.BlockSpec((tm, tn), lambda j, i: (i, j)),
        scratch_shapes=[pltpu.VMEM((k, tn), BF16)],
        compiler_params=_cparams(2),
        name="matmul",
    )(*args)


def _qkv_kernel(a_ref, wq_ref, wk_ref, wv_ref, *refs, with_kmean, n_alias):
    refs = refs[n_alias:]
    if with_kmean:
        q_ref, k_ref, v_ref, kbf_ref, vbf_ref, km_ref, wbf = refs
    else:
        q_ref, k_ref, v_ref, kbf_ref, vbf_ref, wbf = refs

    @pl.when(pl.program_id(1) == 0)
    def _():
        wbf[0] = wq_ref[...].astype(BF16)
        wbf[1] = wk_ref[...].astype(BF16)
        wbf[2] = wv_ref[...].astype(BF16)

    a = a_ref[...]
    q_ref[...] = jnp.dot(a, wbf[0], preferred_element_type=F32)
    k = jnp.dot(a, wbf[1], preferred_element_type=F32)
    v = jnp.dot(a, wbf[2], preferred_element_type=F32)
    k_ref[...] = k
    v_ref[...] = v
    kbf_ref[...] = k.astype(BF16)
    vbf_ref[...] = v.astype(BF16)
    if with_kmean:
        tm, tn = k.shape
        km_ref[...] = jnp.mean(k.reshape(tm // MOBA_BLOCK, MOBA_BLOCK, tn), axis=1)


def _qkv_proj(a, w, layer, col_blocks, with_kmean, n_stack, stacks=None, tn=256, tm=1024):
    m, kdim = a.shape
    tm = min(m, tm)
    nj = D_HEADS // tn

    def wspec(which):
        return pl.BlockSpec((None, kdim, tn), lambda j, i: (layer, 0, col_blocks(j)[which]))

    ospec = pl.BlockSpec((tm, tn), lambda j, i: (i, j))
    sspec = pl.BlockSpec((None, tm, tn), lambda j, i: (layer, i, j))
    flat = jax.ShapeDtypeStruct((m, D_HEADS), F32)
    stacked = jax.ShapeDtypeStruct((n_stack, m, D_HEADS), F32)
    out_shape = [flat, stacked, stacked] + [jax.ShapeDtypeStruct((m, D_HEADS), BF16)] * 2
    out_specs = [ospec, sspec, sspec, ospec, ospec]
    if with_kmean:
        out_shape.append(jax.ShapeDtypeStruct((m // tm, tm // MOBA_BLOCK, D_HEADS), F32))
        out_specs.append(pl.BlockSpec((None, tm // MOBA_BLOCK, tn), lambda j, i: (i, 0, j)))
    in_specs = [pl.BlockSpec((tm, kdim), lambda j, i: (i, 0)), wspec(0), wspec(1), wspec(2)]
    args = [a, w, w, w]
    aliases = {}
    if stacks is not None:
        in_specs += [pl.BlockSpec(memory_space=pl.ANY)] * 2
        args += list(stacks)
        aliases = {4: 1, 5: 2}
    return pl.pallas_call(
        functools.partial(_qkv_kernel, with_kmean=with_kmean, n_alias=len(aliases)),
        out_shape=out_shape,
        grid=(nj, m // tm),
        in_specs=in_specs,
        out_specs=out_specs,
        scratch_shapes=[pltpu.VMEM((3, kdim, tn), BF16)],
        input_output_aliases=aliases,
        compiler_params=_cparams(2),
        name="qkv_proj",
    )(*args)


def _silu(x):
    return x * (1.0 / (1.0 + jnp.exp(-x)))


def _ffn_up_kernel(a_ref, wg_ref, wv_ref, cwg_ref, cwv_ref, cbg_ref, cbv_ref, pg_ref, pv_ref,
                   act_ref, sg_ref, sv_ref, wbf, tail, *, tiles_per_seq, decode):
    i = pl.program_id(1)

    @pl.when(i == 0)
    def _():
        wbf[0] = wg_ref[...].astype(BF16)
        wbf[1] = wv_ref[...].astype(BF16)
        tail[...] = jnp.zeros(tail.shape, F32)

    a = a_ref[...]
    tm = a.shape[0]
    row = lax.broadcasted_iota(I32, (tm, 1), 0)
    outs = []
    for half, (cw_ref, cb_ref, p_ref, s_ref) in enumerate(
            ((cwg_ref, cbg_ref, pg_ref, sg_ref), (cwv_ref, cbv_ref, pv_ref, sv_ref))):
        u = jnp.dot(a, wbf[half], preferred_element_type=F32)
        cw = cw_ref[...]
        if decode:
            p0 = p_ref[:, 0, :]
            p1 = p_ref[:, 1, :]
            c = cb_ref[...] + cw[0:1] * p0 + cw[1:2] * p1 + cw[2:3] * u
            s_ref[:, 0, :] = p1
            s_ref[:, 1, :] = u
        else:
            first = (i % tiles_per_seq) == 0
            up = tail[half]
            tail[half] = u[tm - 8:tm]
            prev = p_ref[0]
            m1 = jnp.where(first, prev[1:2], up[7:8])
            m2 = jnp.where(first, prev[0:1], up[6:7])
            r1 = pltpu.roll(u, 1, axis=0)
            r2 = pltpu.roll(u, 2, axis=0)
            u1 = jnp.where(row == 0, m1, r1)
            u2 = jnp.where(row == 0, m2, jnp.where(row == 1, m1, r2))
            c = cb_ref[...] + cw[0:1] * u2 + cw[1:2] * u1 + cw[2:3] * u

            @pl.when((i % tiles_per_seq) == tiles_per_seq - 1)
            def _():
                s_ref[0] = u[tm - 2:tm]
        outs.append(c)
    act_ref[...] = (_silu(outs[0]) * outs[1]).astype(BF16)


def _ffn_up(a, w_up, conv_w, conv_b3, prev, layer, seq_rows, tn=512, tm=1024):
    m, d = a.shape
    d_ff = w_up.shape[-1] // 2
    n_seq = m // seq_rows
    decode = seq_rows == 1
    tm = m if decode else min(tm, seq_rows)
    tiles_per_seq = 1 if decode else seq_rows // tm
    nj = d_ff // tn

    if decode:
        pspec = lambda off: pl.BlockSpec((n_seq, 2, tn), lambda j, i: (0, 0, j + off))
        sspec = pl.BlockSpec((n_seq, 2, tn), lambda j, i: (0, 0, j))
    else:
        pspec = lambda off: pl.BlockSpec((1, 2, tn), lambda j, i: (i // tiles_per_seq, 0, j + off))
        sspec = pl.BlockSpec((1, 2, tn), lambda j, i: (i // tiles_per_seq, 0, j))
    in_specs = [
        pl.BlockSpec((tm, d), lambda j, i: (i, 0)),
        pl.BlockSpec((None, d, tn), lambda j, i: (layer, 0, j)),
        pl.BlockSpec((None, d, tn), lambda j, i: (layer, 0, j + nj)),
        pl.BlockSpec((None, CONV_W, tn), lambda j, i: (layer, 0, j)),
        pl.BlockSpec((None, CONV_W, tn), lambda j, i: (layer, 0, j + nj)),
        pl.BlockSpec((None, 1, tn), lambda j, i: (layer, 0, j)),
        pl.BlockSpec((None, 1, tn), lambda j, i: (layer, 0, j + nj)),
        pspec(0), pspec(nj),
    ]
    return pl.pallas_call(
        functools.partial(_ffn_up_kernel, tiles_per_seq=tiles_per_seq, decode=decode),
        out_shape=[jax.ShapeDtypeStruct((m, d_ff), BF16),
                   jax.ShapeDtypeStruct((n_seq, 2, d_ff), F32),
                   jax.ShapeDtypeStruct((n_seq, 2, d_ff), F32)],
        grid=(nj, m // tm),
        in_specs=in_specs,
        out_specs=[pl.BlockSpec((tm, tn), lambda j, i: (i, j)), sspec, sspec],
        scratch_shapes=[pltpu.VMEM((2, d, tn), BF16), pltpu.VMEM((2, 8, tn), F32)],
        compiler_params=_cparams(2),
        name="ffn_up",
    )(a, w_up, w_up, conv_w, conv_w, conv_b3, conv_b3, prev, prev)


def _t5_bucket(dist):
    dist = jnp.maximum(dist, 0)
    max_exact = NUM_BUCKETS // 2
    log_ratio = jnp.log(jnp.maximum(dist, 1).astype(F32) / max_exact) / math.log(MAX_DISTANCE / max_exact)
    large = jnp.minimum(max_exact + (log_ratio * (NUM_BUCKETS - max_exact)).astype(I32), NUM_BUCKETS - 1)
    return jnp.where(dist < max_exact, dist, large)


def _t5_lookup(t5_table, dist):
    onehot = (_t5_bucket(dist)[..., None] == jnp.arange(NUM_BUCKETS, dtype=I32)).astype(F32)
    return jnp.einsum("...k,kh->...h", onehot, t5_table.astype(F32), precision=lax.Precision.HIGHEST)


def _t5_near_tiles(t5_table):
    r = jnp.arange(LANE, dtype=I32)
    dist = jnp.arange(2, dtype=I32)[:, None, None] * LANE + r[None, :, None] - r[None, None, :]
    bias = jnp.moveaxis(_t5_lookup(t5_table, dist), -1, 0)
    far = t5_table[NUM_BUCKETS - 1].astype(F32)[:, None, None, None]
    return jnp.where(dist[None] >= 0, bias - far, NEG_INF)


def _fold_lanes(x, op):
    acc = x[:, :LANE]
    for t in range(1, x.shape[1] // LANE):
        acc = op(acc, x[:, t * LANE:(t + 1) * LANE])
    return acc


def _attend(q_bf, k_ref, v_ref, ls_ref, st_ref, i, far_bias, t0, t1):
    def logits(off):
        s = lax.dot_general(q_bf, k_ref[pl.ds(off, CH), :], (((1,), (1,)), ((), ())), preferred_element_type=F32)
        b = far_bias(off)
        return s if b is None else s + b

    st_ref[0] = jnp.full((CH, LANE), -jnp.inf, F32)

    def chunk_offsets(first, count):
        return [pl.multiple_of((first + n) * CH, CH) for n in range(count)]

    def pairs_then_rest(n_chunks, body):
        def pair(c, carry):
            body(chunk_offsets(2 * c, 2))
            return carry
        lax.fori_loop(0, n_chunks // 2, pair, 0)

        @pl.when(n_chunks % 2 == 1)
        def _():
            body(chunk_offsets(n_chunks - 1, 1))

    def far_chunks(offs):
        ss = [logits(off) for off in offs]
        for off, s in zip(offs, ss):
            ls_ref[:, pl.ds(off, CH)] = s
        st_ref[0] = functools.reduce(jnp.maximum, [_fold_lanes(s, jnp.maximum) for s in ss], st_ref[0])

    pairs_then_rest(i, far_chunks)

    if t1 is not None:
        @pl.when(i > 0)
        def _():
            off = pl.multiple_of(i * CH - LANE, LANE)
            sub = ls_ref[0:LANE, pl.ds(off, LANE)] + t1
            ls_ref[0:LANE, pl.ds(off, LANE)] = sub
            st_ref[0, 0:LANE] = jnp.maximum(st_ref[0, 0:LANE], sub)

    off_d = pl.multiple_of(i * CH, CH)
    s = logits(off_d)
    for a in range(SUB):
        rows = slice(a * LANE, (a + 1) * LANE)
        pieces = []
        for b in range(SUB):
            if b > a:
                pieces.append(jnp.full((LANE, LANE), NEG_INF, F32))
                continue
            blk = s[rows, b * LANE:(b + 1) * LANE]
            if b == a:
                blk = blk + t0
            elif b == a - 1 and t1 is not None:
                blk = blk + t1
            pieces.append(blk)
        sa = jnp.concatenate(pieces, axis=1)
        ls_ref[rows, pl.ds(off_d, CH)] = sa
        st_ref[0, rows] = jnp.maximum(st_ref[0, rows], _fold_lanes(sa, jnp.maximum))

    st_ref[0] = jnp.broadcast_to(jnp.max(st_ref[0], axis=1, keepdims=True), (CH, LANE))
    st_ref[1] = jnp.zeros((CH, LANE), F32)
    st_ref[2] = jnp.zeros((CH, LANE), F32)

    def pv_chunks(offs):
        m = st_ref[0]
        lsum, acc = st_ref[1], st_ref[2]
        for off in offs:
            chunk = ls_ref[:, pl.ds(off, CH)]
            ps = [jnp.exp(chunk[:, t * LANE:(t + 1) * LANE] - m) for t in range(SUB)]
            lsum = lsum + functools.reduce(jnp.add, ps)
            p = jnp.concatenate(ps, axis=1).astype(BF16)
            acc = acc + jnp.dot(p, v_ref[pl.ds(off, CH), :], preferred_element_type=F32)
        st_ref[1] = lsum
        st_ref[2] = acc

    pairs_then_rest(i + 1, pv_chunks)
    return st_ref[2] / jnp.sum(st_ref[1], axis=1, keepdims=True)


def _logf_suffix_kernel(fg_ref, b_ref, lf_ref, r_ref):
    z = fg_ref[...] + b_ref[...]
    lf = jnp.minimum(z, 0.0) - jnp.log(1.0 + jnp.exp(-jnp.abs(z)))
    lf_ref[...] = lf
    n = lf.shape[1]
    lane = lax.broadcasted_iota(I32, lf.shape, 1)
    x = lf
    k = 1
    while k < n:
        x = x + jnp.where(lane >= k, pltpu.roll(x, k, axis=1), 0.0)
        k *= 2
    r_ref[...] = x[:, n - 1:n] - x


def _logf_suffix(fg_t, b_col):
    return pl.pallas_call(
        _logf_suffix_kernel,
        out_shape=[jax.ShapeDtypeStruct(fg_t.shape, F32)] * 2,
        compiler_params=pltpu.CompilerParams(vmem_limit_bytes=VMEM_LIMIT),
        name="logf_suffix",
    )(fg_t, b_col)


def _causal_tile():
    row = lax.broadcasted_iota(I32, (LANE, LANE), 0)
    col = lax.broadcasted_iota(I32, (LANE, LANE), 1)
    return jnp.where(col <= row, 0.0, NEG_INF)


def _fox_kernel(q_ref, k_ref, v_ref, rk_ref, rq_ref, o_ref, ls_ref, st_ref):
    i = pl.program_id(2)
    q_bf = (q_ref[...] * SCALE).astype(BF16)
    rq = rq_ref[...]
    far = lambda off: rk_ref[:, pl.ds(off, CH)] - rq
    o = _attend(q_bf, k_ref, v_ref, ls_ref, st_ref, i, far, _causal_tile(), None)
    o_ref[...] = o.astype(o_ref.dtype)


def _fox_prompt(q, kbf, vbf, rk, rq, bsz, seq):
    nq = seq // CH
    return pl.pallas_call(
        _fox_kernel,
        out_shape=jax.ShapeDtypeStruct(q.shape, BF16),
        grid=(bsz, H_C, nq),
        in_specs=[pl.BlockSpec((CH, HEAD_DIM), lambda b, h, i: (b * nq + i, h)),
                  pl.BlockSpec((seq, HEAD_DIM), lambda b, h, i: (b, h)),
                  pl.BlockSpec((seq, HEAD_DIM), lambda b, h, i: (b, h)),
                  pl.BlockSpec((None, None, 1, seq), lambda b, h, i: (b, h, 0, 0)),
                  pl.BlockSpec((None, None, CH, 1), lambda b, h, i: (b, h, i, 0))],
        out_specs=pl.BlockSpec((CH, HEAD_DIM), lambda b, h, i: (b * nq + i, h)),
        scratch_shapes=[pltpu.VMEM((CH, seq), F32), pltpu.VMEM((3, CH, LANE), F32)],
        compiler_params=_cparams(3),
        name="fox_prompt",
    )(q, kbf, vbf, rk, rq)


def _ordered_key(score):
    score = jnp.where(score == 0.0, 0.0, score)
    bits = pltpu.bitcast(score, I32)
    return jnp.where(bits < 0, bits ^ 0x7FFFFFFF, bits)


def _kth_largest_key(count_ge, k, shape):
    def body(step, r_u):
        cand_u = r_u | lax.shift_left(jnp.int32(1), 31 - step)
        return jnp.where(count_ge(cand_u ^ INT_MIN) >= k, cand_u, r_u)

    return lax.fori_loop(0, 32, body, jnp.zeros(shape, I32)) ^ INT_MIN


def _dsa_kernel(q_ref, tail_ref, kidx_ref, k_ref, v_ref, tn_ref, o_ref, work_ref, mb_ref, st_ref, *, k_sel):
    i = pl.program_id(1)
    tail = tail_ref[...]
    iw = H_IDX * D_IDX + D_IDX
    qi = jnp.concatenate([tail[:, h * D_IDX:(h + 1) * D_IDX] for h in range(H_IDX)], axis=0).astype(BF16)
    wi = [tail[:, iw + h:iw + h + 1] for h in range(H_IDX)]
    row = lax.broadcasted_iota(I32, (CH, CH), 0)
    col = lax.broadcasted_iota(I32, (CH, CH), 1)

    def keys_at(rows, off, width):
        return pltpu.bitcast(work_ref[rows, pl.ds(off, width)], I32)

    def score_chunk(off, diagonal):
        kt = kidx_ref[pl.ds(off, CH), :].astype(BF16)
        s = lax.dot_general(qi, kt, (((1,), (1,)), ((), ())), preferred_element_type=F32)
        score = jnp.zeros((CH, CH), F32)
        for h in range(H_IDX):
            score = score + jnp.maximum(s[h * CH:(h + 1) * CH], 0.0) * wi[h]
        if diagonal:
            score = jnp.where(col <= row, score, NEG_INF)
        work_ref[:, pl.ds(off, CH)] = pltpu.bitcast(_ordered_key(score), F32)

    def far_scores(c, carry):
        score_chunk(pl.multiple_of(c * CH, CH), False)
        return carry

    lax.fori_loop(0, i, far_scores, 0)
    off_d = pl.multiple_of(i * CH, CH)
    score_chunk(off_d, True)

    lrow = lax.broadcasted_iota(I32, (LANE, LANE), 0)
    lcol = lax.broadcasted_iota(I32, (LANE, LANE), 1)
    incl = jnp.where(lrow <= lcol, 1.0, 0.0).astype(BF16)

    for a in range(SUB):
        rows = slice(a * LANE, (a + 1) * LANE)

        def count(pred, rows=rows, a=a):
            def body(c, acc):
                blk = keys_at(rows, pl.multiple_of(c * CH, CH), CH)
                for t in range(SUB):
                    acc = acc + jnp.where(pred(blk[:, t * LANE:(t + 1) * LANE]), 1, 0)
                return acc
            acc = lax.fori_loop(0, i, body, jnp.zeros((LANE, LANE), I32))
            blk = keys_at(rows, off_d, (a + 1) * LANE)
            for t in range(a + 1):
                acc = acc + jnp.where(pred(blk[:, t * LANE:(t + 1) * LANE]), 1, 0)
            return jnp.sum(acc, axis=1, keepdims=True)

        thr = _kth_largest_key(lambda t: count(lambda key: key >= t), k_sel, (LANE, 1))
        n_ge = count(lambda key: key >= thr)
        need = (k_sel - count(lambda key: key > thr)).astype(F32)
        all_ties_kept = jnp.max(n_ge.astype(F32)) <= k_sel

        @pl.when(all_ties_kept)
        def _(rows=rows, thr=thr):
            def body(c, carry):
                off = pl.multiple_of(c * CH, CH)
                mb_ref[rows, pl.ds(off, CH)] = jnp.where(keys_at(rows, off, CH) >= thr, 0.0, NEG_INF)
                return carry
            lax.fori_loop(0, i + 1, body, 0)

        @pl.when(jnp.logical_not(all_ties_kept))
        def _(rows=rows, thr=thr, need=need):
            def body(c, seen):
                off = pl.multiple_of(c * LANE, LANE)
                key = keys_at(rows, off, LANE)
                eq = key == thr
                pre = jnp.dot(jnp.where(eq, 1.0, 0.0).astype(BF16), incl, preferred_element_type=F32)
                sel = (key > thr) | (eq & (pre + seen <= need))
                mb_ref[rows, pl.ds(off, LANE)] = jnp.where(sel, 0.0, NEG_INF)
                return seen + pre[:, LANE - 1:LANE]
            lax.fori_loop(0, (i + 1) * SUB, body, jnp.zeros((LANE, 1), F32))

    for h in range(H_A):
        hs = slice(h * HEAD_DIM, (h + 1) * HEAD_DIM)
        q_bf = (q_ref[:, hs] * SCALE).astype(BF16)
        far = lambda off: mb_ref[:, pl.ds(off, CH)]
        o = _attend(q_bf, k_ref.at[:, hs], v_ref.at[:, hs], work_ref, st_ref, i, far, tn_ref[h, 0], tn_ref[h, 1])
        o_ref[:, hs] = o.astype(o_ref.dtype)


def _dsa_prompt(q, tail, kidx, kbf, vbf, tn, bsz, seq):
    nq = seq // CH
    wa = H_A * HEAD_DIM
    k_sel = min(DSA_TOPK, seq // 4)
    once = dict(pipeline_mode=pl.Buffered(1))
    return pl.pallas_call(
        functools.partial(_dsa_kernel, k_sel=k_sel),
        out_shape=jax.ShapeDtypeStruct(q.shape, BF16),
        grid=(bsz, nq),
        in_specs=[pl.BlockSpec((CH, wa), lambda b, i: (b * nq + i, 0)),
                  pl.BlockSpec((CH, tail.shape[1]), lambda b, i: (b * nq + i, 0)),
                  pl.BlockSpec((seq, D_IDX), lambda b, i: (b, 0), **once),
                  pl.BlockSpec((seq, wa), lambda b, i: (b, 0), **once),
                  pl.BlockSpec((seq, wa), lambda b, i: (b, 0), **once),
                  pl.BlockSpec((H_A, 2, LANE, LANE), lambda b, i: (0, 0, 0, 0), **once)],
        out_specs=pl.BlockSpec((CH, wa), lambda b, i: (b * nq + i, 0)),
        scratch_shapes=[pltpu.VMEM((CH, seq), F32), pltpu.VMEM((CH, seq), F32), pltpu.VMEM((3, CH, LANE), F32)],
        compiler_params=_cparams(2),
        name="dsa_prompt",
    )(q, tail, kidx, kbf, vbf, tn)


def _top_blocks(gate, n_valid, n_top):
    nblk = gate.shape[1]
    blk = lax.broadcasted_iota(I32, gate.shape, 1)
    g = jnp.where(blk < n_valid, gate, NEG_INF)
    rank = jnp.zeros(gate.shape, I32)
    for m in range(nblk):
        gm = g[:, m:m + 1]
        rank = rank + jnp.where((gm > g) | ((gm == g) & (m < blk)), 1, 0)
    return jnp.where((blk < n_valid) & (rank < n_top), 1.0, 0.0)


def _moba_kernel(o_in_ref, q_ref, km_ref, k_ref, v_ref, tn_ref, o_ref, ls_ref, st_ref, *, n_top):
    del o_in_ref
    i = pl.program_id(2)
    q = q_ref[...]
    gate = lax.dot_general(q, km_ref[...], (((1,), (1,)), ((), ())), preferred_element_type=F32,
                           precision=lax.Precision.HIGHEST)
    own = (i * CH + lax.broadcasted_iota(I32, (CH, 1), 0)) // MOBA_BLOCK
    blk = lax.broadcasted_iota(I32, gate.shape, 1)
    sel = jnp.where(blk == own, 1.0, _top_blocks(gate, own, n_top))
    q_bf = (q * SCALE).astype(BF16)
    per_ch = CH // MOBA_BLOCK

    def far(off):
        first = off // MOBA_BLOCK
        cols = []
        for t in range(per_ch):
            picked = jnp.sum(jnp.where(blk == first + t, sel, 0.0), axis=1, keepdims=True)
            cols.append(jnp.broadcast_to(jnp.where(picked > 0.5, 0.0, NEG_INF), (CH, MOBA_BLOCK)))
        return jnp.concatenate(cols, axis=1)

    o = _attend(q_bf, k_ref, v_ref, ls_ref, st_ref, i, far, tn_ref[0, 0], tn_ref[0, 1])
    o_ref[...] = o.astype(o_ref.dtype)


def _moba_prompt(o_ab, q, kmean, kbf, vbf, tn, bsz, seq):
    nq = seq // CH
    nblk = kmean.shape[1]
    n_top = min(MOBA_TOPK, nblk)
    return pl.pallas_call(
        functools.partial(_moba_kernel, n_top=n_top),
        out_shape=jax.ShapeDtypeStruct(o_ab.shape, o_ab.dtype),
        grid=(bsz, H_B, nq),
        in_specs=[pl.BlockSpec(memory_space=pl.ANY),
                  pl.BlockSpec((CH, HEAD_DIM), lambda b, h, i: (b * nq + i, H_A + h)),
                  pl.BlockSpec((None, nblk, HEAD_DIM), lambda b, h, i: (b, 0, H_A + h)),
                  pl.BlockSpec((seq, HEAD_DIM), lambda b, h, i: (b, H_A + h)),
                  pl.BlockSpec((seq, HEAD_DIM), lambda b, h, i: (b, H_A + h)),
                  pl.BlockSpec((1, 2, LANE, LANE), lambda b, h, i: (H_A + h, 0, 0, 0))],
        out_specs=pl.BlockSpec((CH, HEAD_DIM), lambda b, h, i: (b * nq + i, H_A + h)),
        scratch_shapes=[pltpu.VMEM((CH, seq), F32), pltpu.VMEM((3, CH, LANE), F32)],
        input_output_aliases={0: 0},
        compiler_params=_cparams(3),
        name="moba_prompt",
    )(o_ab, q, kmean, kbf, vbf, tn)


def _paged_specs(n, block, layer, pages_of):
    def one(r):
        zeros = (0,) * (len(block) - 2)
        return pl.BlockSpec(block, lambda b, g, pt: (layer, pt[b, pages_of(b, g, r)]) + zeros)
    return [one(r) for r in range(n)]


def _idx_scores_kernel(pt_ref, *refs, n_pg):
    del pt_ref
    kp, (qi_ref, wi_ref, o_ref) = refs[:n_pg], refs[n_pg:]
    qi = qi_ref[...].astype(BF16)
    wi = wi_ref[...]
    for r in range(n_pg):
        s = lax.dot_general(qi, kp[r][...].astype(BF16), (((1,), (1,)), ((), ())), preferred_element_type=F32)
        o_ref[r:r + 1, :] = jnp.sum(jnp.maximum(s, 0.0) * wi, axis=0, keepdims=True)


def _idx_scores_sample(cache_kidx, li, page_table, qi8, wi8):
    nb, n_pages = page_table.shape
    page = cache_kidx.shape[2]
    n_pg = min(16, n_pages)
    blk = (None, None, page, D_IDX)
    return pl.pallas_call(
        functools.partial(_idx_scores_kernel, n_pg=n_pg),
        out_shape=jax.ShapeDtypeStruct((nb, n_pages, page), F32),
        grid_spec=pltpu.PrefetchScalarGridSpec(
            num_scalar_prefetch=1,
            grid=(nb, n_pages // n_pg),
            in_specs=_paged_specs(n_pg, blk, li, lambda b, g, r: g * n_pg + r)
            + [pl.BlockSpec((None, 8, D_IDX), lambda b, g, pt: (b, 0, 0)),
               pl.BlockSpec((None, 8, 1), lambda b, g, pt: (b, 0, 0))],
            out_specs=pl.BlockSpec((None, n_pg, page), lambda b, g, pt: (b, g, 0))),
        compiler_params=_cparams(2),
        name="idx_scores_sample",
    )(page_table, *([cache_kidx] * n_pg), qi8, wi8)


def _dsa_select_kernel(sc_ref, qi_ref, wi_ref, kin_ref, mb_ref, mbn_ref, *, k_sel):
    nb, n_pages, page = sc_ref.shape
    incl = jnp.where(lax.broadcasted_iota(I32, (page, page), 0) <= lax.broadcasted_iota(I32, (page, page), 1),
                     1.0, 0.0).astype(BF16)
    before = jnp.where(lax.broadcasted_iota(I32, (n_pages, n_pages), 1) < lax.broadcasted_iota(I32, (n_pages, n_pages), 0),
                       1.0, 0.0).astype(BF16)

    def total(x):
        return jnp.sum(jnp.sum(x, axis=0, keepdims=True), axis=1, keepdims=True)

    for b in range(nb):
        s_new = jnp.sum(qi_ref[b] * kin_ref[b:b + 1, :], axis=1, keepdims=True)
        s_new = jnp.sum(jnp.maximum(s_new, 0.0) * wi_ref[b], axis=0, keepdims=True)
        key = _ordered_key(sc_ref[b])
        key_new = _ordered_key(jnp.broadcast_to(s_new, (8, LANE)))[0:1, 0:1]

        def count_ge(t):
            return total(jnp.where(key >= t, 1, 0)) + jnp.where(key_new >= t, 1, 0)

        thr = _kth_largest_key(count_ge, k_sel, (1, 1))
        gt = key > thr
        need = (k_sel - total(jnp.where(gt, 1, 0)) - jnp.where(key_new > thr, 1, 0)).astype(F32)
        eq = key == thr
        eq_bf = jnp.where(eq, 1.0, 0.0).astype(BF16)
        pre = jnp.dot(eq_bf, incl, preferred_element_type=F32)
        off = jnp.sum(jnp.dot(before, eq_bf, preferred_element_type=F32), axis=1, keepdims=True)
        sel = gt | (eq & (pre + off <= need))
        mb_ref[b] = jnp.where(sel, 0.0, NEG_INF)
        n_eq = total(jnp.where(eq, 1.0, 0.0))
        sel_new = (key_new > thr) | ((key_new == thr) & (n_eq + 1.0 <= need))
        mbn_ref[b:b + 1, :] = jnp.broadcast_to(jnp.where(sel_new, 0.0, NEG_INF), (1, LANE))


def _dsa_select_sample(scores, qi8, wi8, ki_new):
    nb, n_pages, page = scores.shape
    k_sel = min(DSA_TOPK, (n_pages * page + 1) // 4)
    return pl.pallas_call(
        functools.partial(_dsa_select_kernel, k_sel=k_sel),
        out_shape=[jax.ShapeDtypeStruct(scores.shape, F32), jax.ShapeDtypeStruct((nb, LANE), F32)],
        compiler_params=pltpu.CompilerParams(vmem_limit_bytes=VMEM_LIMIT),
        name="dsa_select_sample",
    )(scores, qi8, wi8, ki_new)


def _logf_suffix_sample_kernel(pt_ref, *refs, n_pg):
    del pt_ref
    lp, (fg_ref, bf_ref, r_ref, lfn_ref, carry) = refs[:n_pg], refs[n_pg:]
    page = lp[0].shape[0]

    @pl.when(pl.program_id(1) == 0)
    def _():
        z = fg_ref[...] + bf_ref[...]
        lfn = jnp.minimum(z, 0.0) - jnp.log(1.0 + jnp.exp(-jnp.abs(z)))
        lfn_ref[...] = lfn
        carry[...] = lfn

    row = lax.broadcasted_iota(I32, (page, H_C), 0)
    c = carry[...]
    for r in reversed(range(n_pg)):
        lf = lp[r][...]
        x = lf
        k = 1
        while k < page:
            x = x + jnp.where(row + k < page, pltpu.roll(x, page - k, axis=0), 0.0)
            k *= 2
        r_ref[r * page:(r + 1) * page, :] = x - lf + c
        c = c + x[0:1, :]
    carry[...] = c


def _logf_suffix_sample(cache_logf, li, page_table, fg_new, b_f):
    nb, n_pages = page_table.shape
    page = cache_logf.shape[2]
    n_pg = min(16, n_pages)
    n_steps = n_pages // n_pg
    blk = (None, None, page, H_C)
    return pl.pallas_call(
        functools.partial(_logf_suffix_sample_kernel, n_pg=n_pg),
        out_shape=[jax.ShapeDtypeStruct((nb, n_pages * page, H_C), F32), jax.ShapeDtypeStruct((nb, 1, H_C), F32)],
        grid_spec=pltpu.PrefetchScalarGridSpec(
            num_scalar_prefetch=1,
            grid=(nb, n_steps),
            in_specs=_paged_specs(n_pg, blk, li, lambda b, g, r: (n_steps - 1 - g) * n_pg + r)
            + [pl.BlockSpec((None, 1, H_C), lambda b, g, pt: (b, 0, 0)),
               pl.BlockSpec((1, H_C), lambda b, g, pt: (0, 0))],
            out_specs=[pl.BlockSpec((None, n_pg * page, H_C), lambda b, g, pt: (b, n_steps - 1 - g, 0)),
                       pl.BlockSpec((None, 1, H_C), lambda b, g, pt: (b, 0, 0))],
            scratch_shapes=[pltpu.VMEM((1, H_C), F32)]),
        compiler_params=_cparams(2),
        name="logf_suffix_sample",
    )(page_table, *([cache_logf] * n_pg), fg_new, b_f)


def _decode_partials_kernel(pt_ref, *refs, n_pg, has_mask):
    del pt_ref
    kp, vp, rest = refs[:n_pg], refs[n_pg:2 * n_pg], refs[2 * n_pg:]
    if has_mask:
        qbd_ref, ebd_ref, bias_ref, mask_ref, m_ref, l_ref, acc_ref, ks_ref, kbf = rest
    else:
        qbd_ref, ebd_ref, bias_ref, m_ref, l_ref, acc_ref, ks_ref, kbf = rest
    page = kp[0].shape[0] // H_AB
    heads = [slice(h * HEAD_DIM, (h + 1) * HEAD_DIM) for h in range(H_AB)]
    for r in range(n_pg):
        kt = pltpu.einshape("phd->hpd", kp[r][...].reshape(page, H_AB, HEAD_DIM))
        for h in range(H_AB):
            kbf[r * page:(r + 1) * page, heads[h]] = kt[h].astype(BF16)
    logits = jnp.dot(kbf[...], qbd_ref[...], preferred_element_type=F32)
    s = logits[:, :H_AB] + bias_ref[...]
    if has_mask:
        eye = lax.broadcasted_iota(I32, (page, page), 0) == lax.broadcasted_iota(I32, (page, page), 1)
        first = pl.program_id(1) * n_pg
        cols = [jnp.sum(jnp.where(eye, mask_ref[pl.ds(first + r, 1), :], 0.0), axis=1, keepdims=True)
                for r in range(n_pg)]
        lane = lax.broadcasted_iota(I32, s.shape, 1)
        s = s + jnp.where(lane < H_A, jnp.concatenate(cols, axis=0), 0.0)
    per_blk = MOBA_BLOCK // page
    for blk in range(n_pg // per_blk):
        rows = slice(blk * MOBA_BLOCK, (blk + 1) * MOBA_BLOCK)
        sb = s[rows]
        m = jnp.max(sb, axis=0, keepdims=True)
        p = jnp.exp(sb - m)
        m_ref[blk:blk + 1, :] = m
        l_ref[blk:blk + 1, :] = jnp.sum(p, axis=0, keepdims=True)
        pexp = jnp.dot(p.astype(BF16), ebd_ref[...], preferred_element_type=F32)
        pages = range(blk * per_blk, (blk + 1) * per_blk)
        vt = [pltpu.einshape("phd->hpd", vp[pg][...].reshape(page, H_AB, HEAD_DIM)) for pg in pages]
        for h in range(H_AB):
            acc_ref[blk:blk + 1, heads[h]] = sum(
                jnp.sum(pexp[n * page:(n + 1) * page, heads[h]] * vt[n][h], axis=0, keepdims=True)
                for n in range(per_blk))
        ks_ref[blk] = sum(jnp.sum(kp[pg][...].reshape(page, H_AB, HEAD_DIM), axis=0) for pg in pages)


def _decode_partials(cache_k, cache_v, li, page_table, qbd, ebd, bias, mask):
    nb, n_pages = page_table.shape
    page = cache_k.shape[2] // H_AB
    n_pg = 2 * (MOBA_BLOCK // page)
    n_steps = n_pages // n_pg
    nblk = n_pg * page // MOBA_BLOCK
    rows = n_pg * page
    blk = (None, None, page * H_AB, HEAD_DIM)
    pages_of = lambda b, g, r: g * n_pg + r
    per_b = bias.shape[0] > 1
    in_specs = (_paged_specs(n_pg, blk, li, pages_of) + _paged_specs(n_pg, blk, li, pages_of)
                + [pl.BlockSpec((None, D_HEADS, LANE), lambda b, g, pt: (b, 0, 0)),
                   pl.BlockSpec((H_AB, D_HEADS), lambda b, g, pt: (0, 0)),
                   pl.BlockSpec((None, rows, H_AB), lambda b, g, pt: (b if per_b else 0, g, 0))])
    args = [page_table] + [cache_k] * n_pg + [cache_v] * n_pg + [qbd, ebd, bias]
    if mask is not None:
        in_specs.append(pl.BlockSpec((None, n_pages, page), lambda b, g, pt: (b, 0, 0)))
        args.append(mask)
    small = pl.BlockSpec((None, None, nblk, H_AB), lambda b, g, pt: (b, g, 0, 0))
    wide = pl.BlockSpec((None, None, nblk, D_HEADS), lambda b, g, pt: (b, g, 0, 0))
    wide3 = pl.BlockSpec((None, None, nblk, H_AB, HEAD_DIM), lambda b, g, pt: (b, g, 0, 0, 0))
    m, l, acc, ks = pl.pallas_call(
        functools.partial(_decode_partials_kernel, n_pg=n_pg, has_mask=mask is not None),
        out_shape=[jax.ShapeDtypeStruct((nb, n_steps, nblk, H_AB), F32)] * 2
        + [jax.ShapeDtypeStruct((nb, n_steps, nblk, D_HEADS), F32),
           jax.ShapeDtypeStruct((nb, n_steps, nblk, H_AB, HEAD_DIM), F32)],
        grid_spec=pltpu.PrefetchScalarGridSpec(
            num_scalar_prefetch=1,
            grid=(nb, n_steps),
            in_specs=in_specs,
            out_specs=[small, small, wide, wide3],
            scratch_shapes=[pltpu.VMEM((rows, D_HEADS), BF16)]),
        compiler_params=_cparams(2),
        name="decode_partials",
    )(*args)
    nb_all = n_steps * nblk
    return (m.reshape(nb, nb_all, H_AB), l.reshape(nb, nb_all, H_AB),
            acc.reshape(nb, nb_all, D_HEADS), ks.reshape(nb, nb_all, H_AB, HEAD_DIM))


def _decode_combine_kernel(m_ref, l_ref, acc_ref, ks_ref, q_ref, kn_ref, vn_ref, bn_ref, o_ref, *, moba_from, n_top):
    q = q_ref[...]
    prod = q * kn_ref[...]
    nblk = m_ref.shape[0]
    blk = lax.broadcasted_iota(I32, (nblk, 1), 0)
    for h in range(H_AB):
        hs = slice(h * HEAD_DIM, (h + 1) * HEAD_DIM)
        lg_new = jnp.sum(prod[:, hs], axis=1, keepdims=True) * SCALE + bn_ref[:, h:h + 1]
        m_h = m_ref[:, h:h + 1]
        if h >= moba_from:
            gate = jnp.sum(ks_ref[:, h, :] * (1.0 / MOBA_BLOCK) * q[:, hs], axis=1, keepdims=True)
            keep = jnp.zeros((nblk, 1), jnp.bool_)
            for _ in range(n_top):
                best = jnp.max(gate, axis=0, keepdims=True)
                first = jnp.min(jnp.where(gate == best, blk, nblk), axis=0, keepdims=True)
                keep = keep | (blk == first)
                gate = jnp.where(blk == first, -jnp.inf, gate)
            m_h = jnp.where(keep, m_h, -jnp.inf)
        top = jnp.maximum(jnp.max(m_h, axis=0, keepdims=True), lg_new)
        w = jnp.exp(m_h - top)
        e_new = jnp.exp(lg_new - top)
        den = jnp.sum(w * l_ref[:, h:h + 1], axis=0, keepdims=True) + e_new
        num = jnp.sum(w * acc_ref[:, hs], axis=0, keepdims=True) + e_new * vn_ref[:, hs]
        o_ref[:, hs] = (num / den).astype(o_ref.dtype)


def _decode_combine(m, l, acc, ks, q, k_new, v_new, bias_new, moba_from):
    nb, nblk, _ = m.shape
    n_top = min(MOBA_TOPK, nblk)
    small = pl.BlockSpec((None, nblk, H_AB), lambda b: (b, 0, 0))
    wide = pl.BlockSpec((None, nblk, D_HEADS), lambda b: (b, 0, 0))
    row = pl.BlockSpec((None, 1, D_HEADS), lambda b: (b, 0, 0))
    out = pl.pallas_call(
        functools.partial(_decode_combine_kernel, moba_from=moba_from, n_top=n_top),
        out_shape=jax.ShapeDtypeStruct((nb, 1, D_HEADS), BF16),
        grid=(nb,),
        in_specs=[small, small, wide, pl.BlockSpec((None, nblk, H_AB, HEAD_DIM), lambda b: (b, 0, 0, 0)),
                  row, row, row, pl.BlockSpec((None, 1, LANE), lambda b: (b, 0, 0))],
        out_specs=row,
        compiler_params=_cparams(1),
        name="decode_combine",
    )(m, l, acc, ks, q.reshape(nb, 1, D_HEADS), k_new.reshape(nb, 1, D_HEADS), v_new.reshape(nb, 1, D_HEADS),
      bias_new.reshape(nb, 1, LANE))
    return out.reshape(nb, D_HEADS)


def _ab_cols(tn):
    per = H_A * HEAD_DIM // tn
    def cols(j):
        base = (j // per) * 3 * per + j % per
        return base, base + per, base + 2 * per
    return cols


def _c_cols(tn):
    per = D_HEADS // tn
    return lambda j: (j, j + per, j + 2 * per)


def _block_diag_queries(q):
    head_of_row = jnp.arange(D_HEADS, dtype=I32) // HEAD_DIM
    onehot = (head_of_row[:, None] == jnp.arange(LANE, dtype=I32)[None, :]).astype(F32)
    return ((q * SCALE)[:, :, None] * onehot[None]).astype(BF16)


def kernel(x_prompt, x_sample, cache_k_ab, cache_v_ab, cache_kidx, cache_k_c, cache_v_c, cache_logf_c, state_conv,
           page_table, g_mix, g_ffn, g_final, w_in_ab, w_out_ab, t5_table, w_in_c, b_forget, w_out_c, w_up, conv_w,
           conv_b, w_down):
    bsz, seq, d = x_prompt.shape
    nb, dseq, _ = x_sample.shape
    depth = g_mix.shape[0]
    d_ff = w_down.shape[1]
    n_pages = page_table.shape[1]
    page = cache_k_ab.shape[2]
    past = n_pages * page
    assert dseq == 1 and d == D_HEADS and seq % CH == 0 and past % MOBA_BLOCK == 0
    t_p = bsz * seq
    tn_ffn = 512 if d_ff % 512 == 0 else LANE

    qkv_w = 3 * D_HEADS
    n_tail_ab = w_in_ab.shape[-1] - qkv_w
    n_tail_c = w_in_c.shape[-1] - qkv_w
    w_tail_ab = jnp.pad(w_in_ab[:, :, qkv_w:], ((0, 0), (0, 0), (0, TAIL_PAD - n_tail_ab)))
    w_tail_c = jnp.pad(w_in_c[:, :, qkv_w:], ((0, 0), (0, 0), (0, LANE - n_tail_c)))
    conv_b3 = conv_b.reshape(depth, 1, 2 * d_ff)
    ck_ab, cv_ab, ck_c, cv_c = (c.reshape(c.shape[0], c.shape[1], page * H_AB, HEAD_DIM)
                                for c in (cache_k_ab, cache_v_ab, cache_k_c, cache_v_c))

    tn = _t5_near_tiles(t5_table)
    t5_past = _t5_lookup(t5_table, past - jnp.arange(past, dtype=I32))[None]
    t5_self = jnp.pad(_t5_lookup(t5_table, jnp.zeros((1,), I32)), ((0, 0), (0, LANE - H_AB)))
    lane = jnp.arange(LANE, dtype=I32)[None]
    ebd = (jnp.arange(H_AB, dtype=I32)[:, None] == (jnp.arange(D_HEADS, dtype=I32) // HEAD_DIM)[None, :]).astype(BF16)
    zero_conv = jnp.zeros((bsz, CONV_W - 1, 2 * d_ff), F32)
    iq, ik, iw = H_IDX * D_IDX, H_IDX * D_IDX + D_IDX, H_IDX * D_IDX + D_IDX + H_IDX

    xp = x_prompt.reshape(t_p, d)
    xs = x_sample.reshape(nb, d)
    outs = {name: [] for name in ("kidx_p", "lf_p", "cv_p", "kidx_s", "lf_s", "cv_s")}
    n_ab, n_c = (depth + 1) // 2, depth // 2
    kv_ab_p = kv_ab_s = kv_c_p = kv_c_s = None
    for layer in range(depth):
        li = layer // 2
        hp = _rmsnorm(xp, g_mix[layer], BF16)
        hs = _rmsnorm(xs, g_mix[layer], BF16)
        if layer % 2 == 0:
            q, k_st, v_st, kbf, vbf, kmean = _qkv_proj(hp, w_in_ab, li, _ab_cols(256), True, n_ab, kv_ab_p)
            kv_ab_p = (k_st, v_st)
            tail = _matmul(hp, w_tail_ab, li, TAIL_PAD)
            kidx = tail[:, iq:ik]
            o = _dsa_prompt(q, tail, kidx, kbf, vbf, tn, bsz, seq)
            o = _moba_prompt(o, q, kmean.reshape(bsz, seq // MOBA_BLOCK, D_HEADS), kbf, vbf, tn, bsz, seq)
            xp = _matmul(o, w_out_ab, li, 512, res=xp)
            outs["kidx_p"].append(kidx.reshape(bsz, seq, D_IDX))
            q, k_st, v_st, _, _ = _qkv_proj(hs, w_in_ab, li, _ab_cols(256), False, n_ab, kv_ab_s)
            kv_ab_s = (k_st, v_st)
            k, v = k_st[li], v_st[li]
            tail = _matmul(hs, w_tail_ab, li, TAIL_PAD)
            qi8 = jnp.pad(tail[:, :iq].reshape(nb, H_IDX, D_IDX), ((0, 0), (0, 8 - H_IDX), (0, 0)))
            wi8 = jnp.pad(tail[:, ik:iw].reshape(nb, H_IDX, 1), ((0, 0), (0, 8 - H_IDX), (0, 0)))
            ki_new = tail[:, iq:ik]
            scores = _idx_scores_sample(cache_kidx, li, page_table, qi8, wi8)
            mb, mb_new = _dsa_select_sample(scores, qi8, wi8, ki_new)
            parts = _decode_partials(ck_ab, cv_ab, li, page_table, _block_diag_queries(q), ebd, t5_past,
                                     mb)
            bias_new = t5_self + jnp.where(lane < H_A, mb_new, 0.0)
            o = _decode_combine(*parts, q, k, v, bias_new, H_A)
            xs = _matmul(o, w_out_ab, li, 512, res=xs)
            outs["kidx_s"].append(ki_new.reshape(nb, 1, D_IDX))
        else:
            q, k_st, v_st, kbf, vbf = _qkv_proj(hp, w_in_c, li, _c_cols(256), False, n_c, kv_c_p)
            kv_c_p = (k_st, v_st)
            tail = _matmul(hp, w_tail_c, li, LANE)
            fg_t = jnp.transpose(tail[:, :H_C].reshape(bsz, seq, H_C), (0, 2, 1)).reshape(bsz * H_C, seq)
            lf_t, r_t = _logf_suffix(fg_t, jnp.tile(b_forget[li], bsz).reshape(bsz * H_C, 1))
            o = _fox_prompt(q, kbf, vbf, r_t.reshape(bsz, H_C, 1, seq), r_t.reshape(bsz, H_C, seq, 1), bsz, seq)
            xp = _matmul(o, w_out_c, li, 512, res=xp)
            outs["lf_p"].append(jnp.transpose(lf_t.reshape(bsz, H_C, seq), (0, 2, 1)))
            q, k_st, v_st, _, _ = _qkv_proj(hs, w_in_c, li, _c_cols(256), False, n_c, kv_c_s)
            kv_c_s = (k_st, v_st)
            k, v = k_st[li], v_st[li]
            tail = _matmul(hs, w_tail_c, li, LANE)
            r_past, lf_new = _logf_suffix_sample(cache_logf_c, li, page_table, tail[:, :H_C].reshape(nb, 1, H_C),
                                                 b_forget[li].reshape(1, H_C))
            parts = _decode_partials(ck_c, cv_c, li, page_table, _block_diag_queries(q), ebd, r_past, None)
            o = _decode_combine(*parts, q, k, v, jnp.zeros((nb, LANE), F32), H_C)
            xs = _matmul(o, w_out_c, li, 512, res=xs)
            outs["lf_s"].append(lf_new)
        act, sg, sv = _ffn_up(_rmsnorm(xp, g_ffn[layer], BF16), w_up, conv_w, conv_b3, zero_conv, layer, seq, tn=tn_ffn)
        xp = _matmul(act, w_down, layer, 512, res=xp)
        outs["cv_p"].append(jnp.concatenate([sg, sv], axis=-1))
        act, sg, sv = _ffn_up(_rmsnorm(xs, g_ffn[layer], BF16), w_up, conv_w, conv_b3, state_conv[layer], layer, 1,
                              tn=tn_ffn)
        xs = _matmul(act, w_down, layer, 512, res=xs)
        outs["cv_s"].append(jnp.concatenate([sg, sv], axis=-1))
    y_prompt = _rmsnorm(xp, g_final, F32).reshape(bsz, seq, d)
    y_sample = _rmsnorm(xs, g_final, F32).reshape(nb, 1, d)
    st = lambda name: jnp.stack(outs[name])
    heads_p = lambda a: a.reshape(a.shape[0], bsz, seq, H_AB, HEAD_DIM)
    heads_s = lambda a: a.reshape(a.shape[0], nb, 1, H_AB, HEAD_DIM)
    return (y_prompt, y_sample, heads_p(kv_ab_p[0]), heads_p(kv_ab_p[1]), st("kidx_p"), heads_p(kv_c_p[0]),
            heads_p(kv_c_p[1]), st("lf_p"), st("cv_p"), heads_s(kv_ab_s[0]), heads_s(kv_ab_s[1]), st("kidx_s"),
            heads_s(kv_c_s[0]), heads_s(kv_c_s[1]), st("lf_s"), st("cv_s"))
```

```python
import functools
import math

import jax
import jax.numpy as jnp
import numpy as np
from jax import lax
from jax.experimental import pallas as pl
from jax.experimental.pallas import tpu as pltpu

F32 = jnp.float32
BF16 = jnp.bfloat16
I32 = jnp.int32

HEAD_DIM = 128
H_A = 8
H_B = 8
H_AB = H_A + H_B
H_C = 16
H_IDX = 4
D_IDX = 64
DSA_TOPK = 256
MOBA_BLOCK = 256
MOBA_TOPK = 3
NUM_BUCKETS = 32
MAX_DISTANCE = 128
CONV_W = 3
RMS_EPS = 1e-6
NEG_INF = -1e30
SCALE = HEAD_DIM ** -0.5
D_HEADS = H_AB * HEAD_DIM
TAIL_PAD = 384
LANE = 128
CH = 512
SUB = CH // LANE
INT_MIN = -(2 ** 31)
VMEM_LIMIT = 56 * 1024 * 1024


def _cparams(n_axes):
    return pltpu.CompilerParams(dimension_semantics=("arbitrary",) * n_axes,
                                vmem_limit_bytes=VMEM_LIMIT)


def _rmsnorm_kernel(x_ref, g_ref, o_ref):
    x = x_ref[...]
    y = x * lax.rsqrt(jnp.mean(x * x, axis=-1, keepdims=True) + RMS_EPS)
    o_ref[...] = (y * g_ref[...]).astype(o_ref.dtype)


def _rmsnorm(x, g, out_dtype):
    m, d = x.shape
    tm = min(m, 512)
    return pl.pallas_call(
        _rmsnorm_kernel,
        out_shape=jax.ShapeDtypeStruct((m, d), out_dtype),
        grid=(m // tm,),
        in_specs=[pl.BlockSpec((tm, d), lambda i: (i, 0)),
                  pl.BlockSpec((1, d), lambda i: (0, 0))],
        out_specs=pl.BlockSpec((tm, d), lambda i: (i, 0)),
        compiler_params=_cparams(1),
        name="rmsnorm",
    )(x, g.reshape(1, d))


def _mm_kernel(*refs, has_res):
    if has_res:
        a_ref, w_ref, r_ref, o_ref, wbf = refs
    else:
        a_ref, w_ref, o_ref, wbf = refs

    @pl.when(pl.program_id(1) == 0)
    def _():
        wbf[...] = w_ref[...].astype(BF16)

    acc = jnp.dot(a_ref[...], wbf[...], preferred_element_type=F32)
    if has_res:
        acc = r_ref[...] + acc
    o_ref[...] = acc


def _matmul(a, w, layer, tn, res=None):
    m, k = a.shape
    n = w.shape[-1]
    tm = min(m, 1024 if k <= D_HEADS else 512)
    in_specs = [pl.BlockSpec((tm, k), lambda j, i: (i, 0)),
                pl.BlockSpec((None, k, tn), lambda j, i: (layer, 0, j))]
    args = [a, w]
    if res is not None:
        in_specs.append(pl.BlockSpec((tm, tn), lambda j, i: (i, j)))
        args.append(res)
    return pl.pallas_call(
        functools.partial(_mm_kernel, has_res=res is not None),
        out_shape=jax.ShapeDtypeStruct((m, n), F32),
        grid=(n // tn, m // tm),
        in_specs=in_specs,
        out_specs=pl.BlockSpec((tm, tn), lambda j, i: (i, j)),
        scratch_shapes=[pltpu.VMEM((k, tn), BF16)],
        compiler_params=_cparams(2),
        name="matmul",
    )(*args)


def _qkv_kernel(a_ref, wq_ref, wk_ref, wv_ref, *refs, with_kmean, n_alias):
    refs = refs[n_alias:]
    if with_kmean:
        q_ref, k_ref, v_ref, kbf_ref, vbf_ref, km_ref, wbf = refs
    else:
        q_ref, k_ref, v_ref, kbf_ref, vbf_ref, wbf = refs

    @pl.when(pl.program_id(1) == 0)
    def _():
        wbf[0] = wq_ref[...].astype(BF16)
        wbf[1] = wk_ref[...].astype(BF16)
        wbf[2] = wv_ref[...].astype(BF16)

    a = a_ref[...]
    q_ref[...] = jnp.dot(a, wbf[0], preferred_element_type=F32)
    k = jnp.dot(a, wbf[1], preferred_element_type=F32)
    v = jnp.dot(a, wbf[2], preferred_element_type=F32)
    k_ref[...] = k
    v_ref[...] = v
    kbf_ref[...] = k.astype(BF16)
    vbf_ref[...] = v.astype(BF16)
    if with_kmean:
        tm, tn = k.shape
        km_ref[...] = jnp.mean(k.reshape(tm // MOBA_BLOCK, MOBA_BLOCK, tn), axis=1)


def _qkv_proj(a, w, layer, col_blocks, with_kmean, n_stack, stacks=None, tn=256, tm=1024):
    m, kdim = a.shape
    tm = min(m, tm)
    nj = D_HEADS // tn

    def wspec(which):
        return pl.BlockSpec((None, kdim, tn), lambda j, i: (layer, 0, col_blocks(j)[which]))

    ospec = pl.BlockSpec((tm, tn), lambda j, i: (i, j))
    sspec = pl.BlockSpec((None, tm, tn), lambda j, i: (layer, i, j))
    flat = jax.ShapeDtypeStruct((m, D_HEADS), F32)
    stacked = jax.ShapeDtypeStruct((n_stack, m, D_HEADS), F32)
    out_shape = [flat, stacked, stacked] + [jax.ShapeDtypeStruct((m, D_HEADS), BF16)] * 2
    out_specs = [ospec, sspec, sspec, ospec, ospec]
    if with_kmean:
        out_shape.append(jax.ShapeDtypeStruct((m // tm, tm // MOBA_BLOCK, D_HEADS), F32))
        out_specs.append(pl.BlockSpec((None, tm // MOBA_BLOCK, tn), lambda j, i: (i, 0, j)))
    in_specs = [pl.BlockSpec((tm, kdim), lambda j, i: (i, 0)), wspec(0), wspec(1), wspec(2)]
    args = [a, w, w, w]
    aliases = {}
    if stacks is not None:
        in_specs += [pl.BlockSpec(memory_space=pl.ANY)] * 2
        args += list(stacks)
        aliases = {4: 1, 5: 2}
    return pl.pallas_call(
        functools.partial(_qkv_kernel, with_kmean=with_kmean, n_alias=len(aliases)),
        out_shape=out_shape,
        grid=(nj, m // tm),
        in_specs=in_specs,
        out_specs=out_specs,
        scratch_shapes=[pltpu.VMEM((3, kdim, tn), BF16)],
        input_output_aliases=aliases,
        compiler_params=_cparams(2),
        name="qkv_proj",
    )(*args)


def _silu(x):
    return x * (1.0 / (1.0 + jnp.exp(-x)))


def _ffn_up_kernel(a_ref, wg_ref, wv_ref, cwg_ref, cwv_ref, cbg_ref, cbv_ref, pg_ref, pv_ref,
                   act_ref, sg_ref, sv_ref, wbf, tail, *, tiles_per_seq, decode):
    i = pl.program_id(1)

    @pl.when(i == 0)
    def _():
        wbf[0] = wg_ref[...].astype(BF16)
        wbf[1] = wv_ref[...].astype(BF16)
        tail[...] = jnp.zeros(tail.shape, F32)

    a = a_ref[...]
    tm = a.shape[0]
    row = lax.broadcasted_iota(I32, (tm, 1), 0)
    outs = []
    for half, (cw_ref, cb_ref, p_ref, s_ref) in enumerate(
            ((cwg_ref, cbg_ref, pg_ref, sg_ref), (cwv_ref, cbv_ref, pv_ref, sv_ref))):
        u = jnp.dot(a, wbf[half], preferred_element_type=F32)
        cw = cw_ref[...]
        if decode:
            p0 = p_ref[:, 0, :]
            p1 = p_ref[:, 1, :]
            c = cb_ref[...] + cw[0:1] * p0 + cw[1:2] * p1 + cw[2:3] * u
            s_ref[:, 0, :] = p1
            s_ref[:, 1, :] = u
        else:
            first = (i % tiles_per_seq) == 0
            up = tail[half]
            tail[half] = u[tm - 8:tm]
            prev = p_ref[0]
            m1 = jnp.where(first, prev[1:2], up[7:8])
            m2 = jnp.where(first, prev[0:1], up[6:7])
            r1 = pltpu.roll(u, 1, axis=0)
            r2 = pltpu.roll(u, 2, axis=0)
            u1 = jnp.where(row == 0, m1, r1)
            u2 = jnp.where(row == 0, m2, jnp.where(row == 1, m1, r2))
            c = cb_ref[...] + cw[0:1] * u2 + cw[1:2] * u1 + cw[2:3] * u

            @pl.when((i % tiles_per_seq) == tiles_per_seq - 1)
            def _():
                s_ref[0] = u[tm - 2:tm]
        outs.append(c)
    act_ref[...] = (_silu(outs[0]) * outs[1]).astype(BF16)


def _ffn_up(a, w_up, conv_w, conv_b3, prev, layer, seq_rows, tn=512, tm=1024):
    m, d = a.shape
    d_ff = w_up.shape[-1] // 2
    n_seq = m // seq_rows
    decode = seq_rows == 1
    tm = m if decode else min(tm, seq_rows)
    tiles_per_seq = 1 if decode else seq_rows // tm
    nj = d_ff // tn

    if decode:
        pspec = lambda off: pl.BlockSpec((n_seq, 2, tn), lambda j, i: (0, 0, j + off))
        sspec = pl.BlockSpec((n_seq, 2, tn), lambda j, i: (0, 0, j))
    else:
        pspec = lambda off: pl.BlockSpec((1, 2, tn), lambda j, i: (i // tiles_per_seq, 0, j + off))
        sspec = pl.BlockSpec((1, 2, tn), lambda j, i: (i // tiles_per_seq, 0, j))
    in_specs = [
        pl.BlockSpec((tm, d), lambda j, i: (i, 0)),
        pl.BlockSpec((None, d, tn), lambda j, i: (layer, 0, j)),
        pl.BlockSpec((None, d, tn), lambda j, i: (layer, 0, j + nj)),
        pl.BlockSpec((None, CONV_W, tn), lambda j, i: (layer, 0, j)),
        pl.BlockSpec((None, CONV_W, tn), lambda j, i: (layer, 0, j + nj)),
        pl.BlockSpec((None, 1, tn), lambda j, i: (layer, 0, j)),
        pl.BlockSpec((None, 1, tn), lambda j, i: (layer, 0, j + nj)),
        pspec(0), pspec(nj),
    ]
    return pl.pallas_call(
        functools.partial(_ffn_up_kernel, tiles_per_seq=tiles_per_seq, decode=decode),
        out_shape=[jax.ShapeDtypeStruct((m, d_ff), BF16),
                   jax.ShapeDtypeStruct((n_seq, 2, d_ff), F32),
                   jax.ShapeDtypeStruct((n_seq, 2, d_ff), F32)],
        grid=(nj, m // tm),
        in_specs=in_specs,
        out_specs=[pl.BlockSpec((tm, tn), lambda j, i: (i, j)), sspec, sspec],
        scratch_shapes=[pltpu.VMEM((2, d, tn), BF16), pltpu.VMEM((2, 8, tn), F32)],
        compiler_params=_cparams(2),
        name="ffn_up",
    )(a, w_up, w_up, conv_w, conv_w, conv_b3, conv_b3, prev, prev)


def _t5_bucket(dist):
    dist = jnp.maximum(dist, 0)
    max_exact = NUM_BUCKETS // 2
    log_ratio = jnp.log(jnp.maximum(dist, 1).astype(F32) / max_exact) / math.log(MAX_DISTANCE / max_exact)
    large = jnp.minimum(max_exact + (log_ratio * (NUM_BUCKETS - max_exact)).astype(I32), NUM_BUCKETS - 1)
    return jnp.where(dist < max_exact, dist, large)


def _t5_lookup(t5_table, dist):
    onehot = (_t5_bucket(dist)[..., None] == jnp.arange(NUM_BUCKETS, dtype=I32)).astype(F32)
    return jnp.einsum("...k,kh->...h", onehot, t5_table.astype(F32), precision=lax.Precision.HIGHEST)


def _t5_near_tiles(t5_table):
    r = jnp.arange(LANE, dtype=I32)
    dist = jnp.arange(2, dtype=I32)[:, None, None] * LANE + r[None, :, None] - r[None, None, :]
    bias = jnp.moveaxis(_t5_lookup(t5_table, dist), -1, 0)
    far = t5_table[NUM_BUCKETS - 1].astype(F32)[:, None, None, None]
    return jnp.where(dist[None] >= 0, bias - far, NEG_INF)


def _fold_lanes(x, op):
    acc = x[:, :LANE]
    for t in range(1, x.shape[1] // LANE):
        acc = op(acc, x[:, t * LANE:(t + 1) * LANE])
    return acc


def _attend(q_bf, k_ref, v_ref, ls_ref, st_ref, i, far_bias, t0, t1):
    def logits(off):
        s = lax.dot_general(q_bf, k_ref[pl.ds(off, CH), :], (((1,), (1,)), ((), ())), preferred_element_type=F32)
        b = far_bias(off)
        return s if b is None else s + b

    st_ref[0] = jnp.full((CH, LANE), -jnp.inf, F32)

    def chunk_offsets(first, count):
        return [pl.multiple_of((first + n) * CH, CH) for n in range(count)]

    def pairs_then_rest(n_chunks, body):
        def pair(c, carry):
            body(chunk_offsets(2 * c, 2))
            return carry
        lax.fori_loop(0, n_chunks // 2, pair, 0)

        @pl.when(n_chunks % 2 == 1)
        def _():
            body(chunk_offsets(n_chunks - 1, 1))

    def far_chunks(offs):
        ss = [logits(off) for off in offs]
        for off, s in zip(offs, ss):
            ls_ref[:, pl.ds(off, CH)] = s
        st_ref[0] = functools.reduce(jnp.maximum, [_fold_lanes(s, jnp.maximum) for s in ss], st_ref[0])

    pairs_then_rest(i, far_chunks)

    if t1 is not None:
        @pl.when(i > 0)
        def _():
            off = pl.multiple_of(i * CH - LANE, LANE)
            sub = ls_ref[0:LANE, pl.ds(off, LANE)] + t1
            ls_ref[0:LANE, pl.ds(off, LANE)] = sub
            st_ref[0, 0:LANE] = jnp.maximum(st_ref[0, 0:LANE], sub)

    off_d = pl.multiple_of(i * CH, CH)
    s = logits(off_d)
    for a in range(SUB):
        rows = slice(a * LANE, (a + 1) * LANE)
        pieces = []
        for b in range(SUB):
            if b > a:
                pieces.append(jnp.full((LANE, LANE), NEG_INF, F32))
                continue
            blk = s[rows, b * LANE:(b + 1) * LANE]
            if b == a:
                blk = blk + t0
            elif b == a - 1 and t1 is not None:
                blk = blk + t1
            pieces.append(blk)
        sa = jnp.concatenate(pieces, axis=1)
        ls_ref[rows, pl.ds(off_d, CH)] = sa
        st_ref[0, rows] = jnp.maximum(st_ref[0, rows], _fold_lanes(sa, jnp.maximum))

    st_ref[0] = jnp.broadcast_to(jnp.max(st_ref[0], axis=1, keepdims=True), (CH, LANE))
    st_ref[1] = jnp.zeros((CH, LANE), F32)
    st_ref[2] = jnp.zeros((CH, LANE), F32)

    def pv_chunks(offs):
        m = st_ref[0]
        lsum, acc = st_ref[1], st_ref[2]
        for off in offs:
            chunk = ls_ref[:, pl.ds(off, CH)]
            ps = [jnp.exp(chunk[:, t * LANE:(t + 1) * LANE] - m) for t in range(SUB)]
            lsum = lsum + functools.reduce(jnp.add, ps)
            p = jnp.concatenate(ps, axis=1).astype(BF16)
            acc = acc + jnp.dot(p, v_ref[pl.ds(off, CH), :], preferred_element_type=F32)
        st_ref[1] = lsum
        st_ref[2] = acc

    pairs_then_rest(i + 1, pv_chunks)
    return st_ref[2] / jnp.sum(st_ref[1], axis=1, keepdims=True)


def _logf_suffix_kernel(fg_ref, b_ref, lf_ref, r_ref):
    z = fg_ref[...] + b_ref[...]
    lf = jnp.minimum(z, 0.0) - jnp.log(1.0 + jnp.exp(-jnp.abs(z)))
    lf_ref[...] = lf
    n = lf.shape[1]
    lane = lax.broadcasted_iota(I32, lf.shape, 1)
    x = lf
    k = 1
    while k < n:
        x = x + jnp.where(lane >= k, pltpu.roll(x, k, axis=1), 0.0)
        k *= 2
    r_ref[...] = x[:, n - 1:n] - x


def _logf_suffix(fg_t, b_col):
    return pl.pallas_call(
        _logf_suffix_kernel,
        out_shape=[jax.ShapeDtypeStruct(fg_t.shape, F32)] * 2,
        compiler_params=pltpu.CompilerParams(vmem_limit_bytes=VMEM_LIMIT),
        name="logf_suffix",
    )(fg_t, b_col)


def _causal_tile():
    row = lax.broadcasted_iota(I32, (LANE, LANE), 0)
    col = lax.broadcasted_iota(I32, (LANE, LANE), 1)
    return jnp.where(col <= row, 0.0, NEG_INF)


def _fox_kernel(q_ref, k_ref, v_ref, rk_ref, rq_ref, o_ref, ls_ref, st_ref):
    i = pl.program_id(2)
    q_bf = (q_ref[...] * SCALE).astype(BF16)
    rq = rq_ref[...]
    far = lambda off: rk_ref[:, pl.ds(off, CH)] - rq
    o = _attend(q_bf, k_ref, v_ref, ls_ref, st_ref, i, far, _causal_tile(), None)
    o_ref[...] = o.astype(o_ref.dtype)


def _fox_prompt(q, kbf, vbf, rk, rq, bsz, seq):
    nq = seq // CH
    return pl.pallas_call(
        _fox_kernel,
        out_shape=jax.ShapeDtypeStruct(q.shape, BF16),
        grid=(bsz, H_C, nq),
        in_specs=[pl.BlockSpec((CH, HEAD_DIM), lambda b, h, i: (b * nq + i, h)),
                  pl.BlockSpec((seq, HEAD_DIM), lambda b, h, i: (b, h)),
                  pl.BlockSpec((seq, HEAD_DIM), lambda b, h, i: (b, h)),
                  pl.BlockSpec((None, None, 1, seq), lambda b, h, i: (b, h, 0, 0)),
                  pl.BlockSpec((None, None, CH, 1), lambda b, h, i: (b, h, i, 0))],
        out_specs=pl.BlockSpec((CH, HEAD_DIM), lambda b, h, i: (b * nq + i, h)),
        scratch_shapes=[pltpu.VMEM((CH, seq), F32), pltpu.VMEM((3, CH, LANE), F32)],
        compiler_params=_cparams(3),
        name="fox_prompt",
    )(q, kbf, vbf, rk, rq)


def _ordered_key(score):
    score = jnp.where(score == 0.0, 0.0, score)
    bits = pltpu.bitcast(score, I32)
    return jnp.where(bits < 0, bits ^ 0x7FFFFFFF, bits)


def _kth_largest_key(count_ge, k, shape):
    def body(step, r_u):
        cand_u = r_u | lax.shift_left(jnp.int32(1), 31 - step)
        return jnp.where(count_ge(cand_u ^ INT_MIN) >= k, cand_u, r_u)

    return lax.fori_loop(0, 32, body, jnp.zeros(shape, I32)) ^ INT_MIN


def _dsa_kernel(q_ref, tail_ref, kidx_ref, k_ref, v_ref, tn_ref, o_ref, work_ref, mb_ref, st_ref, *, k_sel):
    i = pl.program_id(1)
    tail = tail_ref[...]
    iw = H_IDX * D_IDX + D_IDX
    qi = jnp.concatenate([tail[:, h * D_IDX:(h + 1) * D_IDX] for h in range(H_IDX)], axis=0).astype(BF16)
    wi = [tail[:, iw + h:iw + h + 1] for h in range(H_IDX)]
    row = lax.broadcasted_iota(I32, (CH, CH), 0)
    col = lax.broadcasted_iota(I32, (CH, CH), 1)

    def keys_at(rows, off, width):
        return pltpu.bitcast(work_ref[rows, pl.ds(off, width)], I32)

    def score_chunk(off, diagonal):
        kt = kidx_ref[pl.ds(off, CH), :].astype(BF16)
        s = lax.dot_general(qi, kt, (((1,), (1,)), ((), ())), preferred_element_type=F32)
        score = jnp.zeros((CH, CH), F32)
        for h in range(H_IDX):
            score = score + jnp.maximum(s[h * CH:(h + 1) * CH], 0.0) * wi[h]
        if diagonal:
            score = jnp.where(col <= row, score, NEG_INF)
        work_ref[:, pl.ds(off, CH)] = pltpu.bitcast(_ordered_key(score), F32)

    def far_scores(c, carry):
        score_chunk(pl.multiple_of(c * CH, CH), False)
        return carry

    lax.fori_loop(0, i, far_scores, 0)
    off_d = pl.multiple_of(i * CH, CH)
    score_chunk(off_d, True)

    lrow = lax.broadcasted_iota(I32, (LANE, LANE), 0)
    lcol = lax.broadcasted_iota(I32, (LANE, LANE), 1)
    incl = jnp.where(lrow <= lcol, 1.0, 0.0).astype(BF16)

    row_blocks = [slice(a * LANE, (a + 1) * LANE) for a in range(SUB)]

    def count(pred):
        def add_tiles(a, off, n_tiles):
            blk = keys_at(row_blocks[a], off, n_tiles * LANE)
            acc = st_ref[0, row_blocks[a]]
            for t in range(n_tiles):
                acc = acc + jnp.where(pred(blk[:, t * LANE:(t + 1) * LANE], a), 1.0, 0.0)
            st_ref[0, row_blocks[a]] = acc

        def body(c, carry):
            for a in range(SUB):
                add_tiles(a, pl.multiple_of(c * CH, CH), SUB)
            return carry

        st_ref[0] = jnp.zeros((CH, LANE), F32)
        lax.fori_loop(0, i, body, 0)
        for a in range(SUB):
            add_tiles(a, off_d, a + 1)
        return jnp.sum(st_ref[0], axis=1, keepdims=True)

    thr_all = _kth_largest_key(lambda t: count(lambda key, a: key >= t[row_blocks[a]]), k_sel, (CH, 1))
    n_ge_all = count(lambda key, a: key >= thr_all[row_blocks[a]])
    need_all = k_sel - count(lambda key, a: key > thr_all[row_blocks[a]])

    for a in range(SUB):
        rows = row_blocks[a]
        thr, need = thr_all[rows], need_all[rows]
        all_ties_kept = jnp.max(n_ge_all[rows]) <= k_sel

        @pl.when(all_ties_kept)
        def _(rows=rows, thr=thr):
            def body(c, carry):
                off = pl.multiple_of(c * CH, CH)
                mb_ref[rows, pl.ds(off, CH)] = jnp.where(keys_at(rows, off, CH) >= thr, 0.0, NEG_INF)
                return carry
            lax.fori_loop(0, i + 1, body, 0)

        @pl.when(jnp.logical_not(all_ties_kept))
        def _(rows=rows, thr=thr, need=need):
            def body(c, seen):
                off = pl.multiple_of(c * LANE, LANE)
                key = keys_at(rows, off, LANE)
                eq = key == thr
                pre = jnp.dot(jnp.where(eq, 1.0, 0.0).astype(BF16), incl, preferred_element_type=F32)
                sel = (key > thr) | (eq & (pre + seen <= need))
                mb_ref[rows, pl.ds(off, LANE)] = jnp.where(sel, 0.0, NEG_INF)
                return seen + pre[:, LANE - 1:LANE]
            lax.fori_loop(0, (i + 1) * SUB, body, jnp.zeros((LANE, 1), F32))

    for h in range(H_A):
        hs = slice(h * HEAD_DIM, (h + 1) * HEAD_DIM)
        q_bf = (q_ref[:, hs] * SCALE).astype(BF16)
        far = lambda off: mb_ref[:, pl.ds(off, CH)]
        o = _attend(q_bf, k_ref.at[:, hs], v_ref.at[:, hs], work_ref, st_ref, i, far, tn_ref[h, 0], tn_ref[h, 1])
        o_ref[:, hs] = o.astype(o_ref.dtype)


def _dsa_prompt(q, tail, kidx, kbf, vbf, tn, bsz, seq):
    nq = seq // CH
    wa = H_A * HEAD_DIM
    k_sel = min(DSA_TOPK, seq // 4)
    once = dict(pipeline_mode=pl.Buffered(1))
    return pl.pallas_call(
        functools.partial(_dsa_kernel, k_sel=k_sel),
        out_shape=jax.ShapeDtypeStruct(q.shape, BF16),
        grid=(bsz, nq),
        in_specs=[pl.BlockSpec((CH, wa), lambda b, i: (b * nq + i, 0)),
                  pl.BlockSpec((CH, tail.shape[1]), lambda b, i: (b * nq + i, 0)),
                  pl.BlockSpec((seq, D_IDX), lambda b, i: (b, 0), **once),
                  pl.BlockSpec((seq, wa), lambda b, i: (b, 0), **once),
                  pl.BlockSpec((seq, wa), lambda b, i: (b, 0), **once),
                  pl.BlockSpec((H_A, 2, LANE, LANE), lambda b, i: (0, 0, 0, 0), **once)],
        out_specs=pl.BlockSpec((CH, wa), lambda b, i: (b * nq + i, 0)),
        scratch_shapes=[pltpu.VMEM((CH, seq), F32), pltpu.VMEM((CH, seq), F32), pltpu.VMEM((3, CH, LANE), F32)],
        compiler_params=_cparams(2),
        name="dsa_prompt",
    )(q, tail, kidx, kbf, vbf, tn)


def _top_blocks(gate, n_valid, n_top):
    nblk = gate.shape[1]
    blk = lax.broadcasted_iota(I32, gate.shape, 1)
    g = jnp.where(blk < n_valid, gate, NEG_INF)
    rank = jnp.zeros(gate.shape, I32)
    for m in range(nblk):
        gm = g[:, m:m + 1]
        later = jnp.where(m < blk, 1, 0)
        rank = rank + jnp.where(gm > g, 1, 0) + jnp.where(gm == g, later, 0)
    return jnp.where((blk < n_valid) & (rank < n_top), 1.0, 0.0)


def _moba_kernel(o_in_ref, q_ref, km_ref, k_ref, v_ref, tn_ref, o_ref, ls_ref, st_ref, *, n_top):
    del o_in_ref
    i = pl.program_id(2)
    q = q_ref[...]
    gate = lax.dot_general(q, km_ref[...], (((1,), (1,)), ((), ())), preferred_element_type=F32,
                           precision=lax.Precision.HIGHEST)
    own = (i * CH + lax.broadcasted_iota(I32, (CH, 1), 0)) // MOBA_BLOCK
    blk = lax.broadcasted_iota(I32, gate.shape, 1)
    sel = jnp.where(blk == own, 1.0, _top_blocks(gate, own, n_top))
    q_bf = (q * SCALE).astype(BF16)
    per_ch = CH // MOBA_BLOCK

    def far(off):
        first = off // MOBA_BLOCK
        cols = []
        for t in range(per_ch):
            picked = jnp.sum(jnp.where(blk == first + t, sel, 0.0), axis=1, keepdims=True)
            cols.append(jnp.broadcast_to(jnp.where(picked > 0.5, 0.0, NEG_INF), (CH, MOBA_BLOCK)))
        return jnp.concatenate(cols, axis=1)

    o = _attend(q_bf, k_ref, v_ref, ls_ref, st_ref, i, far, tn_ref[0, 0], tn_ref[0, 1])
    o_ref[...] = o.astype(o_ref.dtype)


def _moba_prompt(o_ab, q, kmean, kbf, vbf, tn, bsz, seq):
    nq = seq // CH
    nblk = kmean.shape[1]
    n_top = min(MOBA_TOPK, nblk)
    return pl.pallas_call(
        functools.partial(_moba_kernel, n_top=n_top),
        out_shape=jax.ShapeDtypeStruct(o_ab.shape, o_ab.dtype),
        grid=(bsz, H_B, nq),
        in_specs=[pl.BlockSpec(memory_space=pl.ANY),
                  pl.BlockSpec((CH, HEAD_DIM), lambda b, h, i: (b * nq + i, H_A + h)),
                  pl.BlockSpec((None, nblk, HEAD_DIM), lambda b, h, i: (b, 0, H_A + h)),
                  pl.BlockSpec((seq, HEAD_DIM), lambda b, h, i: (b, H_A + h)),
                  pl.BlockSpec((seq, HEAD_DIM), lambda b, h, i: (b, H_A + h)),
                  pl.BlockSpec((1, 2, LANE, LANE), lambda b, h, i: (H_A + h, 0, 0, 0))],
        out_specs=pl.BlockSpec((CH, HEAD_DIM), lambda b, h, i: (b * nq + i, H_A + h)),
        scratch_shapes=[pltpu.VMEM((CH, seq), F32), pltpu.VMEM((3, CH, LANE), F32)],
        input_output_aliases={0: 0},
        compiler_params=_cparams(3),
        name="moba_prompt",
    )(o_ab, q, kmean, kbf, vbf, tn)


def _paged_specs(n, block, layer, pages_of):
    def one(r):
        zeros = (0,) * (len(block) - 2)
        return pl.BlockSpec(block, lambda b, g, pt: (layer, pt[b, pages_of(b, g, r)]) + zeros)
    return [one(r) for r in range(n)]


def _idx_scores_kernel(pt_ref, *refs, n_pg):
    del pt_ref
    kp, (qi_ref, wi_ref, o_ref) = refs[:n_pg], refs[n_pg:]
    qi = qi_ref[...].astype(BF16)
    wi = wi_ref[...]
    for r in range(n_pg):
        s = lax.dot_general(qi, kp[r][...].astype(BF16), (((1,), (1,)), ((), ())), preferred_element_type=F32)
        o_ref[r:r + 1, :] = jnp.sum(jnp.maximum(s, 0.0) * wi, axis=0, keepdims=True)


def _idx_scores_sample(cache_kidx, li, page_table, qi8, wi8):
    nb, n_pages = page_table.shape
    page = cache_kidx.shape[2]
    n_pg = min(16, n_pages)
    blk = (None, None, page, D_IDX)
    return pl.pallas_call(
        functools.partial(_idx_scores_kernel, n_pg=n_pg),
        out_shape=jax.ShapeDtypeStruct((nb, n_pages, page), F32),
        grid_spec=pltpu.PrefetchScalarGridSpec(
            num_scalar_prefetch=1,
            grid=(nb, n_pages // n_pg),
            in_specs=_paged_specs(n_pg, blk, li, lambda b, g, r: g * n_pg + r)
            + [pl.BlockSpec((None, 8, D_IDX), lambda b, g, pt: (b, 0, 0)),
               pl.BlockSpec((None, 8, 1), lambda b, g, pt: (b, 0, 0))],
            out_specs=pl.BlockSpec((None, n_pg, page), lambda b, g, pt: (b, g, 0))),
        compiler_params=_cparams(2),
        name="idx_scores_sample",
    )(page_table, *([cache_kidx] * n_pg), qi8, wi8)


def _dsa_select_kernel(sc_ref, qi_ref, wi_ref, kin_ref, mb_ref, mbn_ref, *, k_sel):
    nb, n_pages, page = sc_ref.shape
    incl = jnp.where(lax.broadcasted_iota(I32, (page, page), 0) <= lax.broadcasted_iota(I32, (page, page), 1),
                     1.0, 0.0).astype(BF16)
    before = jnp.where(lax.broadcasted_iota(I32, (n_pages, n_pages), 1) < lax.broadcasted_iota(I32, (n_pages, n_pages), 0),
                       1.0, 0.0).astype(BF16)

    def total(x):
        return jnp.sum(jnp.sum(x, axis=0, keepdims=True), axis=1, keepdims=True)

    for b in range(nb):
        s_new = jnp.sum(qi_ref[b] * kin_ref[b:b + 1, :], axis=1, keepdims=True)
        s_new = jnp.sum(jnp.maximum(s_new, 0.0) * wi_ref[b], axis=0, keepdims=True)
        key = _ordered_key(sc_ref[b])
        key_new = _ordered_key(jnp.broadcast_to(s_new, (8, LANE)))[0:1, 0:1]

        def count_ge(t):
            return total(jnp.where(key >= t, 1, 0)) + jnp.where(key_new >= t, 1, 0)

        thr = _kth_largest_key(count_ge, k_sel, (1, 1))
        gt = key > thr
        need = (k_sel - total(jnp.where(gt, 1, 0)) - jnp.where(key_new > thr, 1, 0)).astype(F32)
        eq = key == thr
        eq_bf = jnp.where(eq, 1.0, 0.0).astype(BF16)
        pre = jnp.dot(eq_bf, incl, preferred_element_type=F32)
        off = jnp.sum(jnp.dot(before, eq_bf, preferred_element_type=F32), axis=1, keepdims=True)
        sel = gt | (eq & (pre + off <= need))
        mb_ref[b] = jnp.where(sel, 0.0, NEG_INF)
        n_eq = total(jnp.where(eq, 1.0, 0.0))
        sel_new = (key_new > thr) | ((key_new == thr) & (n_eq + 1.0 <= need))
        mbn_ref[b:b + 1, :] = jnp.broadcast_to(jnp.where(sel_new, 0.0, NEG_INF), (1, LANE))


def _dsa_select_sample(scores, qi8, wi8, ki_new):
    nb, n_pages, page = scores.shape
    k_sel = min(DSA_TOPK, (n_pages * page + 1) // 4)
    return pl.pallas_call(
        functools.partial(_dsa_select_kernel, k_sel=k_sel),
        out_shape=[jax.ShapeDtypeStruct(scores.shape, F32), jax.ShapeDtypeStruct((nb, LANE), F32)],
        compiler_params=pltpu.CompilerParams(vmem_limit_bytes=VMEM_LIMIT),
        name="dsa_select_sample",
    )(scores, qi8, wi8, ki_new)


def _logf_suffix_sample_kernel(pt_ref, *refs, n_pg):
    del pt_ref
    lp, (fg_ref, bf_ref, r_ref, lfn_ref, carry) = refs[:n_pg], refs[n_pg:]
    page = lp[0].shape[0]

    @pl.when(pl.program_id(1) == 0)
    def _():
        z = fg_ref[...] + bf_ref[...]
        lfn = jnp.minimum(z, 0.0) - jnp.log(1.0 + jnp.exp(-jnp.abs(z)))
        lfn_ref[...] = lfn
        carry[...] = lfn

    row = lax.broadcasted_iota(I32, (page, H_C), 0)
    c = carry[...]
    for r in reversed(range(n_pg)):
        lf = lp[r][...]
        x = lf
        k = 1
        while k < page:
            x = x + jnp.where(row + k < page, pltpu.roll(x, page - k, axis=0), 0.0)
            k *= 2
        r_ref[r * page:(r + 1) * page, :] = x - lf + c
        c = c + x[0:1, :]
    carry[...] = c


def _logf_suffix_sample(cache_logf, li, page_table, fg_new, b_f):
    nb, n_pages = page_table.shape
    page = cache_logf.shape[2]
    n_pg = min(16, n_pages)
    n_steps = n_pages // n_pg
    blk = (None, None, page, H_C)
    return pl.pallas_call(
        functools.partial(_logf_suffix_sample_kernel, n_pg=n_pg),
        out_shape=[jax.ShapeDtypeStruct((nb, n_pages * page, H_C), F32), jax.ShapeDtypeStruct((nb, 1, H_C), F32)],
        grid_spec=pltpu.PrefetchScalarGridSpec(
            num_scalar_prefetch=1,
            grid=(nb, n_steps),
            in_specs=_paged_specs(n_pg, blk, li, lambda b, g, r: (n_steps - 1 - g) * n_pg + r)
            + [pl.BlockSpec((None, 1, H_C), lambda b, g, pt: (b, 0, 0)),
               pl.BlockSpec((1, H_C), lambda b, g, pt: (0, 0))],
            out_specs=[pl.BlockSpec((None, n_pg * page, H_C), lambda b, g, pt: (b, n_steps - 1 - g, 0)),
                       pl.BlockSpec((None, 1, H_C), lambda b, g, pt: (b, 0, 0))],
            scratch_shapes=[pltpu.VMEM((1, H_C), F32)]),
        compiler_params=_cparams(2),
        name="logf_suffix_sample",
    )(page_table, *([cache_logf] * n_pg), fg_new, b_f)


def _decode_partials_kernel(pt_ref, *refs, n_pg, has_mask):
    del pt_ref
    kp, vp, rest = refs[:n_pg], refs[n_pg:2 * n_pg], refs[2 * n_pg:]
    if has_mask:
        qbd_ref, ebd_ref, bias_ref, mask_ref, m_ref, l_ref, acc_ref, ks_ref, kbf = rest
    else:
        qbd_ref, ebd_ref, bias_ref, m_ref, l_ref, acc_ref, ks_ref, kbf = rest
    page = kp[0].shape[0] // H_AB
    heads = [slice(h * HEAD_DIM, (h + 1) * HEAD_DIM) for h in range(H_AB)]
    for r in range(n_pg):
        kt = pltpu.einshape("phd->hpd", kp[r][...].reshape(page, H_AB, HEAD_DIM))
        for h in range(H_AB):
            kbf[r * page:(r + 1) * page, heads[h]] = kt[h].astype(BF16)
    logits = jnp.dot(kbf[...], qbd_ref[...], preferred_element_type=F32)
    s = logits[:, :H_AB] + bias_ref[...]
    if has_mask:
        eye = lax.broadcasted_iota(I32, (page, page), 0) == lax.broadcasted_iota(I32, (page, page), 1)
        first = pl.program_id(1) * n_pg
        cols = [jnp.sum(jnp.where(eye, mask_ref[pl.ds(first + r, 1), :], 0.0), axis=1, keepdims=True)
                for r in range(n_pg)]
        lane = lax.broadcasted_iota(I32, s.shape, 1)
        s = s + jnp.where(lane < H_A, jnp.concatenate(cols, axis=0), 0.0)
    per_blk = MOBA_BLOCK // page
    for blk in range(n_pg // per_blk):
        rows = slice(blk * MOBA_BLOCK, (blk + 1) * MOBA_BLOCK)
        sb = s[rows]
        m = jnp.max(sb, axis=0, keepdims=True)
        p = jnp.exp(sb - m)
        m_ref[blk:blk + 1, :] = m
        l_ref[blk:blk + 1, :] = jnp.sum(p, axis=0, keepdims=True)
        pexp = jnp.dot(p.astype(BF16), ebd_ref[...], preferred_element_type=F32)
        pages = range(blk * per_blk, (blk + 1) * per_blk)
        vt = [pltpu.einshape("phd->hpd", vp[pg][...].reshape(page, H_AB, HEAD_DIM)) for pg in pages]
        for h in range(H_AB):
            acc_ref[blk:blk + 1, heads[h]] = sum(
                jnp.sum(pexp[n * page:(n + 1) * page, heads[h]] * vt[n][h], axis=0, keepdims=True)
                for n in range(per_blk))
        ks_ref[blk] = sum(jnp.sum(kp[pg][...].reshape(page, H_AB, HEAD_DIM), axis=0) for pg in pages)


def _decode_partials(cache_k, cache_v, li, page_table, qbd, ebd, bias, mask):
    nb, n_pages = page_table.shape
    page = cache_k.shape[2] // H_AB
    n_pg = 2 * (MOBA_BLOCK // page)
    n_steps = n_pages // n_pg
    nblk = n_pg * page // MOBA_BLOCK
    rows = n_pg * page
    blk = (None, None, page * H_AB, HEAD_DIM)
    pages_of = lambda b, g, r: g * n_pg + r
    per_b = bias.shape[0] > 1
    in_specs = (_paged_specs(n_pg, blk, li, pages_of) + _paged_specs(n_pg, blk, li, pages_of)
                + [pl.BlockSpec((None, D_HEADS, LANE), lambda b, g, pt: (b, 0, 0)),
                   pl.BlockSpec((H_AB, D_HEADS), lambda b, g, pt: (0, 0)),
                   pl.BlockSpec((None, rows, H_AB), lambda b, g, pt: (b if per_b else 0, g, 0))])
    args = [page_table] + [cache_k] * n_pg + [cache_v] * n_pg + [qbd, ebd, bias]
    if mask is not None:
        in_specs.append(pl.BlockSpec((None, n_pages, page), lambda b, g, pt: (b, 0, 0)))
        args.append(mask)
    small = pl.BlockSpec((None, None, nblk, H_AB), lambda b, g, pt: (b, g, 0, 0))
    wide = pl.BlockSpec((None, None, nblk, D_HEADS), lambda b, g, pt: (b, g, 0, 0))
    wide3 = pl.BlockSpec((None, None, nblk, H_AB, HEAD_DIM), lambda b, g, pt: (b, g, 0, 0, 0))
    m, l, acc, ks = pl.pallas_call(
        functools.partial(_decode_partials_kernel, n_pg=n_pg, has_mask=mask is not None),
        out_shape=[jax.ShapeDtypeStruct((nb, n_steps, nblk, H_AB), F32)] * 2
        + [jax.ShapeDtypeStruct((nb, n_steps, nblk, D_HEADS), F32),
           jax.ShapeDtypeStruct((nb, n_steps, nblk, H_AB, HEAD_DIM), F32)],
        grid_spec=pltpu.PrefetchScalarGridSpec(
            num_scalar_prefetch=1,
            grid=(nb, n_steps),
            in_specs=in_specs,
            out_specs=[small, small, wide, wide3],
            scratch_shapes=[pltpu.VMEM((rows, D_HEADS), BF16)]),
        compiler_params=_cparams(2),
        name="decode_partials",
    )(*args)
    nb_all = n_steps * nblk
    return (m.reshape(nb, nb_all, H_AB), l.reshape(nb, nb_all, H_AB),
            acc.reshape(nb, nb_all, D_HEADS), ks.reshape(nb, nb_all, H_AB, HEAD_DIM))


def _decode_combine_kernel(m_ref, l_ref, acc_ref, ks_ref, q_ref, kn_ref, vn_ref, bn_ref, o_ref, *, moba_from, n_top):
    q = q_ref[...]
    prod = q * kn_ref[...]
    nblk = m_ref.shape[0]
    blk = lax.broadcasted_iota(I32, (nblk, 1), 0)
    for h in range(H_AB):
        hs = slice(h * HEAD_DIM, (h + 1) * HEAD_DIM)
        lg_new = jnp.sum(prod[:, hs], axis=1, keepdims=True) * SCALE + bn_ref[:, h:h + 1]
        m_h = m_ref[:, h:h + 1]
        if h >= moba_from:
            gate = jnp.sum(ks_ref[:, h, :] * (1.0 / MOBA_BLOCK) * q[:, hs], axis=1, keepdims=True)
            keep = jnp.zeros((nblk, 1), jnp.bool_)
            for _ in range(n_top):
                best = jnp.max(gate, axis=0, keepdims=True)
                first = jnp.min(jnp.where(gate == best, blk, nblk), axis=0, keepdims=True)
                keep = keep | (blk == first)
                gate = jnp.where(blk == first, -jnp.inf, gate)
            m_h = jnp.where(keep, m_h, -jnp.inf)
        top = jnp.maximum(jnp.max(m_h, axis=0, keepdims=True), lg_new)
        w = jnp.exp(m_h - top)
        e_new = jnp.exp(lg_new - top)
        den = jnp.sum(w * l_ref[:, h:h + 1], axis=0, keepdims=True) + e_new
        num = jnp.sum(w * acc_ref[:, hs], axis=0, keepdims=True) + e_new * vn_ref[:, hs]
        o_ref[:, hs] = (num / den).astype(o_ref.dtype)


def _decode_combine(m, l, acc, ks, q, k_new, v_new, bias_new, moba_from):
    nb, nblk, _ = m.shape
    n_top = min(MOBA_TOPK, nblk)
    small = pl.BlockSpec((None, nblk, H_AB), lambda b: (b, 0, 0))
    wide = pl.BlockSpec((None, nblk, D_HEADS), lambda b: (b, 0, 0))
    row = pl.BlockSpec((None, 1, D_HEADS), lambda b: (b, 0, 0))
    out = pl.pallas_call(
        functools.partial(_decode_combine_kernel, moba_from=moba_from, n_top=n_top),
        out_shape=jax.ShapeDtypeStruct((nb, 1, D_HEADS), BF16),
        grid=(nb,),
        in_specs=[small, small, wide, pl.BlockSpec((None, nblk, H_AB, HEAD_DIM), lambda b: (b, 0, 0, 0)),
                  row, row, row, pl.BlockSpec((None, 1, LANE), lambda b: (b, 0, 0))],
        out_specs=row,
        compiler_params=_cparams(1),
        name="decode_combine",
    )(m, l, acc, ks, q.reshape(nb, 1, D_HEADS), k_new.reshape(nb, 1, D_HEADS), v_new.reshape(nb, 1, D_HEADS),
      bias_new.reshape(nb, 1, LANE))
    return out.reshape(nb, D_HEADS)


def _ab_cols(tn):
    per = H_A * HEAD_DIM // tn
    def cols(j):
        base = (j // per) * 3 * per + j % per
        return base, base + per, base + 2 * per
    return cols


def _c_cols(tn):
    per = D_HEADS // tn
    return lambda j: (j, j + per, j + 2 * per)


def _block_diag_queries(q):
    head_of_row = jnp.arange(D_HEADS, dtype=I32) // HEAD_DIM
    onehot = (head_of_row[:, None] == jnp.arange(LANE, dtype=I32)[None, :]).astype(F32)
    return ((q * SCALE)[:, :, None] * onehot[None]).astype(BF16)


def kernel(x_prompt, x_sample, cache_k_ab, cache_v_ab, cache_kidx, cache_k_c, cache_v_c, cache_logf_c, state_conv,
           page_table, g_mix, g_ffn, g_final, w_in_ab, w_out_ab, t5_table, w_in_c, b_forget, w_out_c, w_up, conv_w,
           conv_b, w_down):
    bsz, seq, d = x_prompt.shape
    nb, dseq, _ = x_sample.shape
    depth = g_mix.shape[0]
    d_ff = w_down.shape[1]
    n_pages = page_table.shape[1]
    page = cache_k_ab.shape[2]
    past = n_pages * page
    assert dseq == 1 and d == D_HEADS and seq % CH == 0 and past % MOBA_BLOCK == 0
    t_p = bsz * seq
    tn_ffn = 512 if d_ff % 512 == 0 else LANE

    qkv_w = 3 * D_HEADS
    n_tail_ab = w_in_ab.shape[-1] - qkv_w
    n_tail_c = w_in_c.shape[-1] - qkv_w
    w_tail_ab = jnp.pad(w_in_ab[:, :, qkv_w:], ((0, 0), (0, 0), (0, TAIL_PAD - n_tail_ab)))
    w_tail_c = jnp.pad(w_in_c[:, :, qkv_w:], ((0, 0), (0, 0), (0, LANE - n_tail_c)))
    conv_b3 = conv_b.reshape(depth, 1, 2 * d_ff)
    ck_ab, cv_ab, ck_c, cv_c = (c.reshape(c.shape[0], c.shape[1], page * H_AB, HEAD_DIM)
                                for c in (cache_k_ab, cache_v_ab, cache_k_c, cache_v_c))

    tn = _t5_near_tiles(t5_table)
    t5_past = _t5_lookup(t5_table, past - jnp.arange(past, dtype=I32))[None]
    t5_self = jnp.pad(_t5_lookup(t5_table, jnp.zeros((1,), I32)), ((0, 0), (0, LANE - H_AB)))
    lane = jnp.arange(LANE, dtype=I32)[None]
    ebd = (jnp.arange(H_AB, dtype=I32)[:, None] == (jnp.arange(D_HEADS, dtype=I32) // HEAD_DIM)[None, :]).astype(BF16)
    zero_conv = jnp.zeros((bsz, CONV_W - 1, 2 * d_ff), F32)
    iq, ik, iw = H_IDX * D_IDX, H_IDX * D_IDX + D_IDX, H_IDX * D_IDX + D_IDX + H_IDX

    xp = x_prompt.reshape(t_p, d)
    xs = x_sample.reshape(nb, d)
    outs = {name: [] for name in ("kidx_p", "lf_p", "cv_p", "kidx_s", "lf_s", "cv_s")}
    n_ab, n_c = (depth + 1) // 2, depth // 2
    kv_ab_p = kv_ab_s = kv_c_p = kv_c_s = None
    for layer in range(depth):
        li = layer // 2
        hp = _rmsnorm(xp, g_mix[layer], BF16)
        hs = _rmsnorm(xs, g_mix[layer], BF16)
        if layer % 2 == 0:
            q, k_st, v_st, kbf, vbf, kmean = _qkv_proj(hp, w_in_ab, li, _ab_cols(256), True, n_ab, kv_ab_p)
            kv_ab_p = (k_st, v_st)
            tail = _matmul(hp, w_tail_ab, li, TAIL_PAD)
            kidx = tail[:, iq:ik]
            o = _dsa_prompt(q, tail, kidx, kbf, vbf, tn, bsz, seq)
            o = _moba_prompt(o, q, kmean.reshape(bsz, seq // MOBA_BLOCK, D_HEADS), kbf, vbf, tn, bsz, seq)
            xp = _matmul(o, w_out_ab, li, 512, res=xp)
            outs["kidx_p"].append(kidx.reshape(bsz, seq, D_IDX))
            q, k_st, v_st, _, _ = _qkv_proj(hs, w_in_ab, li, _ab_cols(256), False, n_ab, kv_ab_s)
            kv_ab_s = (k_st, v_st)
            k, v = k_st[li], v_st[li]
            tail = _matmul(hs, w_tail_ab, li, TAIL_PAD)
            qi8 = jnp.pad(tail[:, :iq].reshape(nb, H_IDX, D_IDX), ((0, 0), (0, 8 - H_IDX), (0, 0)))
            wi8 = jnp.pad(tail[:, ik:iw].reshape(nb, H_IDX, 1), ((0, 0), (0, 8 - H_IDX), (0, 0)))
            ki_new = tail[:, iq:ik]
            scores = _idx_scores_sample(cache_kidx, li, page_table, qi8, wi8)
            mb, mb_new = _dsa_select_sample(scores, qi8, wi8, ki_new)
            parts = _decode_partials(ck_ab, cv_ab, li, page_table, _block_diag_queries(q), ebd, t5_past,
                                     mb)
            bias_new = t5_self + jnp.where(lane < H_A, mb_new, 0.0)
            o = _decode_combine(*parts, q, k, v, bias_new, H_A)
            xs = _matmul(o, w_out_ab, li, 512, res=xs)
            outs["kidx_s"].append(ki_new.reshape(nb, 1, D_IDX))
        else:
            q, k_st, v_st, kbf, vbf = _qkv_proj(hp, w_in_c, li, _c_cols(256), False, n_c, kv_c_p)
            kv_c_p = (k_st, v_st)
            tail = _matmul(hp, w_tail_c, li, LANE)
            fg_t = jnp.transpose(tail[:, :H_C].reshape(bsz, seq, H_C), (0, 2, 1)).reshape(bsz * H_C, seq)
            lf_t, r_t = _logf_suffix(fg_t, jnp.tile(b_forget[li], bsz).reshape(bsz * H_C, 1))
            o = _fox_prompt(q, kbf, vbf, r_t.reshape(bsz, H_C, 1, seq), r_t.reshape(bsz, H_C, seq, 1), bsz, seq)
            xp = _matmul(o, w_out_c, li, 512, res=xp)
            outs["lf_p"].append(jnp.transpose(lf_t.reshape(bsz, H_C, seq), (0, 2, 1)))
            q, k_st, v_st, _, _ = _qkv_proj(hs, w_in_c, li, _c_cols(256), False, n_c, kv_c_s)
            kv_c_s = (k_st, v_st)
            k, v = k_st[li], v_st[li]
            tail = _matmul(hs, w_tail_c, li, LANE)
            r_past, lf_new = _logf_suffix_sample(cache_logf_c, li, page_table, tail[:, :H_C].reshape(nb, 1, H_C),
                                                 b_forget[li].reshape(1, H_C))
            parts = _decode_partials(ck_c, cv_c, li, page_table, _block_diag_queries(q), ebd, r_past, None)
            o = _decode_combine(*parts, q, k, v, jnp.zeros((nb, LANE), F32), H_C)
            xs = _matmul(o, w_out_c, li, 512, res=xs)
            outs["lf_s"].append(lf_new)
        act, sg, sv = _ffn_up(_rmsnorm(xp, g_ffn[layer], BF16), w_up, conv_w, conv_b3, zero_conv, layer, seq, tn=tn_ffn)
        xp = _matmul(act, w_down, layer, 512, res=xp)
        outs["cv_p"].append(jnp.concatenate([sg, sv], axis=-1))
        act, sg, sv = _ffn_up(_rmsnorm(xs, g_ffn[layer], BF16), w_up, conv_w, conv_b3, state_conv[layer], layer, 1,
                              tn=tn_ffn)
        xs = _matmul(act, w_down, layer, 512, res=xs)
        outs["cv_s"].append(jnp.concatenate([sg, sv], axis=-1))
    y_prompt = _rmsnorm(xp, g_final, F32).reshape(bsz, seq, d)
    y_sample = _rmsnorm(xs, g_final, F32).reshape(nb, 1, d)
    st = lambda name: jnp.stack(outs[name])
    heads_p = lambda a: a.reshape(a.shape[0], bsz, seq, H_AB, HEAD_DIM)
    heads_s = lambda a: a.reshape(a.shape[0], nb, 1, H_AB, HEAD_DIM)
    return (y_prompt, y_sample, heads_p(kv_ab_p[0]), heads_p(kv_ab_p[1]), st("kidx_p"), heads_p(kv_c_p[0]),
            heads_p(kv_c_p[1]), st("lf_p"), st("cv_p"), heads_s(kv_ab_s[0]), heads_s(kv_ab_s[1]), st("kidx_s"),
            heads_s(kv_c_s[0]), heads_s(kv_c_s[1]), st("lf_s"), st("cv_s"))
```

```python
import functools
import math

import jax
import jax.numpy as jnp
import numpy as np
from jax import lax
from jax.experimental import pallas as pl
from jax.experimental.pallas import tpu as pltpu

F32 = jnp.float32
BF16 = jnp.bfloat16
I32 = jnp.int32

HEAD_DIM = 128
H_A = 8
H_B = 8
H_AB = H_A + H_B
H_C = 16
H_IDX = 4
D_IDX = 64
DSA_TOPK = 256
MOBA_BLOCK = 256
MOBA_TOPK = 3
NUM_BUCKETS = 32
MAX_DISTANCE = 128
CONV_W = 3
RMS_EPS = 1e-6
NEG_INF = -1e30
SCALE = HEAD_DIM ** -0.5
D_HEADS = H_AB * HEAD_DIM
TAIL_PAD = 384
LANE = 128
CH = 512
SUB = CH // LANE
INT_MIN = -(2 ** 31)
VMEM_LIMIT = 56 * 1024 * 1024


def _cparams(n_axes):
    return pltpu.CompilerParams(dimension_semantics=("arbitrary",) * n_axes,
                                vmem_limit_bytes=VMEM_LIMIT)


def _rmsnorm_kernel(x_ref, g_ref, o_ref):
    x = x_ref[...]
    y = x * lax.rsqrt(jnp.mean(x * x, axis=-1, keepdims=True) + RMS_EPS)
    o_ref[...] = (y * g_ref[...]).astype(o_ref.dtype)


def _rmsnorm(x, g, out_dtype):
    m, d = x.shape
    tm = min(m, 512)
    return pl.pallas_call(
        _rmsnorm_kernel,
        out_shape=jax.ShapeDtypeStruct((m, d), out_dtype),
        grid=(m // tm,),
        in_specs=[pl.BlockSpec((tm, d), lambda i: (i, 0)),
                  pl.BlockSpec((1, d), lambda i: (0, 0))],
        out_specs=pl.BlockSpec((tm, d), lambda i: (i, 0)),
        compiler_params=_cparams(1),
        name="rmsnorm",
    )(x, g.reshape(1, d))


def _mm_kernel(*refs, has_res):
    if has_res:
        a_ref, w_ref, r_ref, o_ref, wbf = refs
    else:
        a_ref, w_ref, o_ref, wbf = refs

    @pl.when(pl.program_id(1) == 0)
    def _():
        wbf[...] = w_ref[...].astype(BF16)

    acc = jnp.dot(a_ref[...], wbf[...], preferred_element_type=F32)
    if has_res:
        acc = r_ref[...] + acc
    o_ref[...] = acc


def _matmul(a, w, layer, tn, res=None):
    m, k = a.shape
    n = w.shape[-1]
    tm = min(m, 1024 if k <= D_HEADS else 512)
    in_specs = [pl.BlockSpec((tm, k), lambda j, i: (i, 0)),
                pl.BlockSpec((None, k, tn), lambda j, i: (layer, 0, j))]
    args = [a, w]
    if res is not None:
        in_specs.append(pl.BlockSpec((tm, tn), lambda j, i: (i, j)))
        args.append(res)
    return pl.pallas_call(
        functools.partial(_mm_kernel, has_res=res is not None),
        out_shape=jax.ShapeDtypeStruct((m, n), F32),
        grid=(n // tn, m // tm),
        in_specs=in_specs,
        out_specs=pl.BlockSpec((tm, tn), lambda j, i: (i, j)),
        scratch_shapes=[pltpu.VMEM((k, tn), BF16)],
        compiler_params=_cparams(2),
        name="matmul",
    )(*args)


def _qkv_kernel(a_ref, wq_ref, wk_ref, wv_ref, *refs, with_kmean, n_alias):
    refs = refs[n_alias:]
    if with_kmean:
        q_ref, k_ref, v_ref, kbf_ref, vbf_ref, km_ref, wbf = refs
    else:
        q_ref, k_ref, v_ref, kbf_ref, vbf_ref, wbf = refs

    @pl.when(pl.program_id(1) == 0)
    def _():
        wbf[0] = wq_ref[...].astype(BF16)
        wbf[1] = wk_ref[...].astype(BF16)
        wbf[2] = wv_ref[...].astype(BF16)

    a = a_ref[...]
    q_ref[...] = jnp.dot(a, wbf[0], preferred_element_type=F32)
    k = jnp.dot(a, wbf[1], preferred_element_type=F32)
    v = jnp.dot(a, wbf[2], preferred_element_type=F32)
    k_ref[...] = k
    v_ref[...] = v
    kbf_ref[...] = k.astype(BF16)
    vbf_ref[...] = v.astype(BF16)
    if with_kmean:
        tm, tn = k.shape
        km_ref[...] = jnp.mean(k.reshape(tm // MOBA_BLOCK, MOBA_BLOCK, tn), axis=1)


def _qkv_proj(a, w, layer, col_blocks, with_kmean, n_stack, stacks=None, tn=256, tm=1024):
    m, kdim = a.shape
    tm = min(m, tm)
    nj = D_HEADS // tn

    def wspec(which):
        return pl.BlockSpec((None, kdim, tn), lambda j, i: (layer, 0, col_blocks(j)[which]))

    ospec = pl.BlockSpec((tm, tn), lambda j, i: (i, j))
    sspec = pl.BlockSpec((None, tm, tn), lambda j, i: (layer, i, j))
    flat = jax.ShapeDtypeStruct((m, D_HEADS), F32)
    stacked = jax.ShapeDtypeStruct((n_stack, m, D_HEADS), F32)
    out_shape = [flat, stacked, stacked] + [jax.ShapeDtypeStruct((m, D_HEADS), BF16)] * 2
    out_specs = [ospec, sspec, sspec, ospec, ospec]
    if with_kmean:
        out_shape.append(jax.ShapeDtypeStruct((m // tm, tm // MOBA_BLOCK, D_HEADS), F32))
        out_specs.append(pl.BlockSpec((None, tm // MOBA_BLOCK, tn), lambda j, i: (i, 0, j)))
    in_specs = [pl.BlockSpec((tm, kdim), lambda j, i: (i, 0)), wspec(0), wspec(1), wspec(2)]
    args = [a, w, w, w]
    aliases = {}
    if stacks is not None:
        in_specs += [pl.BlockSpec(memory_space=pl.ANY)] * 2
        args += list(stacks)
        aliases = {4: 1, 5: 2}
    return pl.pallas_call(
        functools.partial(_qkv_kernel, with_kmean=with_kmean, n_alias=len(aliases)),
        out_shape=out_shape,
        grid=(nj, m // tm),
        in_specs=in_specs,
        out_specs=out_specs,
        scratch_shapes=[pltpu.VMEM((3, kdim, tn), BF16)],
        input_output_aliases=aliases,
        compiler_params=_cparams(2),
        name="qkv_proj",
    )(*args)


def _silu(x):
    return x * (1.0 / (1.0 + jnp.exp(-x)))


def _ffn_up_kernel(a_ref, wg_ref, wv_ref, cwg_ref, cwv_ref, cbg_ref, cbv_ref, pg_ref, pv_ref,
                   act_ref, sg_ref, sv_ref, wbf, tail, *, tiles_per_seq, decode):
    i = pl.program_id(1)

    @pl.when(i == 0)
    def _():
        wbf[0] = wg_ref[...].astype(BF16)
        wbf[1] = wv_ref[...].astype(BF16)
        tail[...] = jnp.zeros(tail.shape, F32)

    a = a_ref[...]
    tm = a.shape[0]
    row = lax.broadcasted_iota(I32, (tm, 1), 0)
    outs = []
    for half, (cw_ref, cb_ref, p_ref, s_ref) in enumerate(
            ((cwg_ref, cbg_ref, pg_ref, sg_ref), (cwv_ref, cbv_ref, pv_ref, sv_ref))):
        u = jnp.dot(a, wbf[half], preferred_element_type=F32)
        cw = cw_ref[...]
        if decode:
            p0 = p_ref[:, 0, :]
            p1 = p_ref[:, 1, :]
            c = cb_ref[...] + cw[0:1] * p0 + cw[1:2] * p1 + cw[2:3] * u
            s_ref[:, 0, :] = p1
            s_ref[:, 1, :] = u
        else:
            first = (i % tiles_per_seq) == 0
            up = tail[half]
            tail[half] = u[tm - 8:tm]
            prev = p_ref[0]
            m1 = jnp.where(first, prev[1:2], up[7:8])
            m2 = jnp.where(first, prev[0:1], up[6:7])
            r1 = pltpu.roll(u, 1, axis=0)
            r2 = pltpu.roll(u, 2, axis=0)
            u1 = jnp.where(row == 0, m1, r1)
            u2 = jnp.where(row == 0, m2, jnp.where(row == 1, m1, r2))
            c = cb_ref[...] + cw[0:1] * u2 + cw[1:2] * u1 + cw[2:3] * u

            @pl.when((i % tiles_per_seq) == tiles_per_seq - 1)
            def _():
                s_ref[0] = u[tm - 2:tm]
        outs.append(c)
    act_ref[...] = (_silu(outs[0]) * outs[1]).astype(BF16)


def _ffn_up(a, w_up, conv_w, conv_b3, prev, layer, seq_rows, tn=512, tm=1024):
    m, d = a.shape
    d_ff = w_up.shape[-1] // 2
    n_seq = m // seq_rows
    decode = seq_rows == 1
    tm = m if decode else min(tm, seq_rows)
    tiles_per_seq = 1 if decode else seq_rows // tm
    nj = d_ff // tn

    if decode:
        pspec = lambda off: pl.BlockSpec((n_seq, 2, tn), lambda j, i: (0, 0, j + off))
        sspec = pl.BlockSpec((n_seq, 2, tn), lambda j, i: (0, 0, j))
    else:
        pspec = lambda off: pl.BlockSpec((1, 2, tn), lambda j, i: (i // tiles_per_seq, 0, j + off))
        sspec = pl.BlockSpec((1, 2, tn), lambda j, i: (i // tiles_per_seq, 0, j))
    in_specs = [
        pl.BlockSpec((tm, d), lambda j, i: (i, 0)),
        pl.BlockSpec((None, d, tn), lambda j, i: (layer, 0, j)),
        pl.BlockSpec((None, d, tn), lambda j, i: (layer, 0, j + nj)),
        pl.BlockSpec((None, CONV_W, tn), lambda j, i: (layer, 0, j)),
        pl.BlockSpec((None, CONV_W, tn), lambda j, i: (layer, 0, j + nj)),
        pl.BlockSpec((None, 1, tn), lambda j, i: (layer, 0, j)),
        pl.BlockSpec((None, 1, tn), lambda j, i: (layer, 0, j + nj)),
        pspec(0), pspec(nj),
    ]
    return pl.pallas_call(
        functools.partial(_ffn_up_kernel, tiles_per_seq=tiles_per_seq, decode=decode),
        out_shape=[jax.ShapeDtypeStruct((m, d_ff), BF16),
                   jax.ShapeDtypeStruct((n_seq, 2, d_ff), F32),
                   jax.ShapeDtypeStruct((n_seq, 2, d_ff), F32)],
        grid=(nj, m // tm),
        in_specs=in_specs,
        out_specs=[pl.BlockSpec((tm, tn), lambda j, i: (i, j)), sspec, sspec],
        scratch_shapes=[pltpu.VMEM((2, d, tn), BF16), pltpu.VMEM((2, 8, tn), F32)],
        compiler_params=_cparams(2),
        name="ffn_up",
    )(a, w_up, w_up, conv_w, conv_w, conv_b3, conv_b3, prev, prev)


def _t5_bucket(dist):
    dist = jnp.maximum(dist, 0)
    max_exact = NUM_BUCKETS // 2
    log_ratio = jnp.log(jnp.maximum(dist, 1).astype(F32) / max_exact) / math.log(MAX_DISTANCE / max_exact)
    large = jnp.minimum(max_exact + (log_ratio * (NUM_BUCKETS - max_exact)).astype(I32), NUM_BUCKETS - 1)
    return jnp.where(dist < max_exact, dist, large)


def _t5_lookup(t5_table, dist):
    onehot = (_t5_bucket(dist)[..., None] == jnp.arange(NUM_BUCKETS, dtype=I32)).astype(F32)
    return jnp.einsum("...k,kh->...h", onehot, t5_table.astype(F32), precision=lax.Precision.HIGHEST)


def _t5_near_tiles(t5_table):
    r = jnp.arange(LANE, dtype=I32)
    dist = jnp.arange(2, dtype=I32)[:, None, None] * LANE + r[None, :, None] - r[None, None, :]
    bias = jnp.moveaxis(_t5_lookup(t5_table, dist), -1, 0)
    far = t5_table[NUM_BUCKETS - 1].astype(F32)[:, None, None, None]
    return jnp.where(dist[None] >= 0, bias - far, NEG_INF)


def _fold_lanes(x, op):
    acc = x[:, :LANE]
    for t in range(1, x.shape[1] // LANE):
        acc = op(acc, x[:, t * LANE:(t + 1) * LANE])
    return acc


def _attend(q_bf, k_ref, v_ref, ls_ref, st_ref, i, far_bias, t0, t1):
    def logits(off):
        s = lax.dot_general(q_bf, k_ref[pl.ds(off, CH), :], (((1,), (1,)), ((), ())), preferred_element_type=F32)
        b = far_bias(off)
        return s if b is None else s + b

    st_ref[0] = jnp.full((CH, LANE), -jnp.inf, F32)

    def chunk_offsets(first, count):
        return [pl.multiple_of((first + n) * CH, CH) for n in range(count)]

    def pairs_then_rest(n_chunks, body):
        def pair(c, carry):
            body(chunk_offsets(2 * c, 2))
            return carry
        lax.fori_loop(0, n_chunks // 2, pair, 0)

        @pl.when(n_chunks % 2 == 1)
        def _():
            body(chunk_offsets(n_chunks - 1, 1))

    def far_chunks(offs):
        ss = [logits(off) for off in offs]
        for off, s in zip(offs, ss):
            ls_ref[:, pl.ds(off, CH)] = s
        st_ref[0] = functools.reduce(jnp.maximum, [_fold_lanes(s, jnp.maximum) for s in ss], st_ref[0])

    pairs_then_rest(i, far_chunks)

    if t1 is not None:
        @pl.when(i > 0)
        def _():
            off = pl.multiple_of(i * CH - LANE, LANE)
            sub = ls_ref[0:LANE, pl.ds(off, LANE)] + t1
            ls_ref[0:LANE, pl.ds(off, LANE)] = sub
            st_ref[0, 0:LANE] = jnp.maximum(st_ref[0, 0:LANE], sub)

    off_d = pl.multiple_of(i * CH, CH)
    s = logits(off_d)
    for a in range(SUB):
        rows = slice(a * LANE, (a + 1) * LANE)
        pieces = []
        for b in range(SUB):
            if b > a:
                pieces.append(jnp.full((LANE, LANE), NEG_INF, F32))
                continue
            blk = s[rows, b * LANE:(b + 1) * LANE]
            if b == a:
                blk = blk + t0
            elif b == a - 1 and t1 is not None:
                blk = blk + t1
            pieces.append(blk)
        sa = jnp.concatenate(pieces, axis=1)
        ls_ref[rows, pl.ds(off_d, CH)] = sa
        st_ref[0, rows] = jnp.maximum(st_ref[0, rows], _fold_lanes(sa, jnp.maximum))

    st_ref[0] = jnp.broadcast_to(jnp.max(st_ref[0], axis=1, keepdims=True), (CH, LANE))
    st_ref[1] = jnp.zeros((CH, LANE), F32)
    st_ref[2] = jnp.zeros((CH, LANE), F32)

    def pv_chunks(offs):
        m = st_ref[0]
        lsum, acc = st_ref[1], st_ref[2]
        for off in offs:
            chunk = ls_ref[:, pl.ds(off, CH)]
            ps = [jnp.exp(chunk[:, t * LANE:(t + 1) * LANE] - m) for t in range(SUB)]
            lsum = lsum + functools.reduce(jnp.add, ps)
            p = jnp.concatenate(ps, axis=1).astype(BF16)
            acc = acc + jnp.dot(p, v_ref[pl.ds(off, CH), :], preferred_element_type=F32)
        st_ref[1] = lsum
        st_ref[2] = acc

    pairs_then_rest(i + 1, pv_chunks)
    return st_ref[2] / jnp.sum(st_ref[1], axis=1, keepdims=True)


def _logf_suffix_kernel(fg_ref, b_ref, lf_ref, r_ref):
    z = fg_ref[...] + b_ref[...]
    lf = jnp.minimum(z, 0.0) - jnp.log(1.0 + jnp.exp(-jnp.abs(z)))
    lf_ref[...] = lf
    n = lf.shape[1]
    lane = lax.broadcasted_iota(I32, lf.shape, 1)
    x = lf
    k = 1
    while k < n:
        x = x + jnp.where(lane >= k, pltpu.roll(x, k, axis=1), 0.0)
        k *= 2
    r_ref[...] = x[:, n - 1:n] - x


def _logf_suffix(fg_t, b_col):
    return pl.pallas_call(
        _logf_suffix_kernel,
        out_shape=[jax.ShapeDtypeStruct(fg_t.shape, F32)] * 2,
        compiler_params=pltpu.CompilerParams(vmem_limit_bytes=VMEM_LIMIT),
        name="logf_suffix",
    )(fg_t, b_col)


def _causal_tile():
    row = lax.broadcasted_iota(I32, (LANE, LANE), 0)
    col = lax.broadcasted_iota(I32, (LANE, LANE), 1)
    return jnp.where(col <= row, 0.0, NEG_INF)


def _fox_kernel(q_ref, k_ref, v_ref, rk_ref, rq_ref, o_ref, ls_ref, st_ref):
    i = pl.program_id(2)
    q_bf = (q_ref[...] * SCALE).astype(BF16)
    rq = rq_ref[...]
    far = lambda off: rk_ref[:, pl.ds(off, CH)] - rq
    o = _attend(q_bf, k_ref, v_ref, ls_ref, st_ref, i, far, _causal_tile(), None)
    o_ref[...] = o.astype(o_ref.dtype)


def _fox_prompt(q, kbf, vbf, rk, rq, bsz, seq):
    nq = seq // CH
    return pl.pallas_call(
        _fox_kernel,
        out_shape=jax.ShapeDtypeStruct(q.shape, BF16),
        grid=(bsz, H_C, nq),
        in_specs=[pl.BlockSpec((CH, HEAD_DIM), lambda b, h, i: (b * nq + i, h)),
                  pl.BlockSpec((seq, HEAD_DIM), lambda b, h, i: (b, h)),
                  pl.BlockSpec((seq, HEAD_DIM), lambda b, h, i: (b, h)),
                  pl.BlockSpec((None, None, 1, seq), lambda b, h, i: (b, h, 0, 0)),
                  pl.BlockSpec((None, None, CH, 1), lambda b, h, i: (b, h, i, 0))],
        out_specs=pl.BlockSpec((CH, HEAD_DIM), lambda b, h, i: (b * nq + i, h)),
        scratch_shapes=[pltpu.VMEM((CH, seq), F32), pltpu.VMEM((3, CH, LANE), F32)],
        compiler_params=_cparams(3),
        name="fox_prompt",
    )(q, kbf, vbf, rk, rq)


def _ordered_key(score):
    score = jnp.where(score == 0.0, 0.0, score)
    bits = pltpu.bitcast(score, I32)
    return jnp.where(bits < 0, bits ^ 0x7FFFFFFF, bits)


def _kth_largest_key(count_ge, k, shape):
    def body(step, r_u):
        cand_u = r_u | lax.shift_left(jnp.int32(1), 31 - step)
        return jnp.where(count_ge(cand_u ^ INT_MIN) >= k, cand_u, r_u)

    return lax.fori_loop(0, 32, body, jnp.zeros(shape, I32)) ^ INT_MIN


def _dsa_kernel(q_ref, tail_ref, kidx_ref, k_ref, v_ref, tn_ref, o_ref, work_ref, mb_ref, st_ref, *, k_sel):
    i = pl.program_id(1)
    tail = tail_ref[...]
    iw = H_IDX * D_IDX + D_IDX
    qi = jnp.concatenate([tail[:, h * D_IDX:(h + 1) * D_IDX] for h in range(H_IDX)], axis=0).astype(BF16)
    wi = [tail[:, iw + h:iw + h + 1] for h in range(H_IDX)]
    row = lax.broadcasted_iota(I32, (CH, CH), 0)
    col = lax.broadcasted_iota(I32, (CH, CH), 1)

    def keys_at(rows, off, width):
        return pltpu.bitcast(work_ref[rows, pl.ds(off, width)], I32)

    def score_chunk(off, diagonal):
        kt = kidx_ref[pl.ds(off, CH), :].astype(BF16)
        s = lax.dot_general(qi, kt, (((1,), (1,)), ((), ())), preferred_element_type=F32)
        score = jnp.zeros((CH, CH), F32)
        for h in range(H_IDX):
            score = score + jnp.maximum(s[h * CH:(h + 1) * CH], 0.0) * wi[h]
        if diagonal:
            score = jnp.where(col <= row, score, NEG_INF)
        work_ref[:, pl.ds(off, CH)] = pltpu.bitcast(_ordered_key(score), F32)

    def far_scores(c, carry):
        score_chunk(pl.multiple_of(c * CH, CH), False)
        return carry

    lax.fori_loop(0, i, far_scores, 0)
    off_d = pl.multiple_of(i * CH, CH)
    score_chunk(off_d, True)

    lrow = lax.broadcasted_iota(I32, (LANE, LANE), 0)
    lcol = lax.broadcasted_iota(I32, (LANE, LANE), 1)
    incl = jnp.where(lrow <= lcol, 1.0, 0.0).astype(BF16)

    row_blocks = [slice(a * LANE, (a + 1) * LANE) for a in range(SUB)]

    def count(pred):
        def add_tiles(a, off, n_tiles):
            blk = keys_at(row_blocks[a], off, n_tiles * LANE)
            acc = st_ref[0, row_blocks[a]]
            for t in range(n_tiles):
                acc = acc + jnp.where(pred(blk[:, t * LANE:(t + 1) * LANE], a), 1.0, 0.0)
            st_ref[0, row_blocks[a]] = acc

        def body(c, carry):
            for a in range(SUB):
                add_tiles(a, pl.multiple_of(c * CH, CH), SUB)
            return carry

        st_ref[0] = jnp.zeros((CH, LANE), F32)
        lax.fori_loop(0, i, body, 0)
        for a in range(SUB):
            add_tiles(a, off_d, a + 1)
        return jnp.sum(st_ref[0], axis=1, keepdims=True)

    thr_all = _kth_largest_key(lambda t: count(lambda key, a: key >= t[row_blocks[a]]), k_sel, (CH, 1))
    n_ge_all = count(lambda key, a: key >= thr_all[row_blocks[a]])
    need_all = k_sel - count(lambda key, a: key > thr_all[row_blocks[a]])

    for a in range(SUB):
        rows = row_blocks[a]
        thr, need = thr_all[rows], need_all[rows]
        all_ties_kept = jnp.max(n_ge_all[rows]) <= k_sel

        @pl.when(all_ties_kept)
        def _(rows=rows, thr=thr):
            def body(c, carry):
                off = pl.multiple_of(c * CH, CH)
                mb_ref[rows, pl.ds(off, CH)] = jnp.where(keys_at(rows, off, CH) >= thr, 0.0, NEG_INF)
                return carry
            lax.fori_loop(0, i + 1, body, 0)

        @pl.when(jnp.logical_not(all_ties_kept))
        def _(rows=rows, thr=thr, need=need):
            def body(c, seen):
                off = pl.multiple_of(c * LANE, LANE)
                key = keys_at(rows, off, LANE)
                eq = key == thr
                pre = jnp.dot(jnp.where(eq, 1.0, 0.0).astype(BF16), incl, preferred_element_type=F32)
                sel = (key > thr) | (eq & (pre + seen <= need))
                mb_ref[rows, pl.ds(off, LANE)] = jnp.where(sel, 0.0, NEG_INF)
                return seen + pre[:, LANE - 1:LANE]
            lax.fori_loop(0, (i + 1) * SUB, body, jnp.zeros((LANE, 1), F32))

    for h in range(H_A):
        hs = slice(h * HEAD_DIM, (h + 1) * HEAD_DIM)
        q_bf = (q_ref[:, hs] * SCALE).astype(BF16)
        far = lambda off: mb_ref[:, pl.ds(off, CH)]
        o = _attend(q_bf, k_ref.at[:, hs], v_ref.at[:, hs], work_ref, st_ref, i, far, tn_ref[h, 0], tn_ref[h, 1])
        o_ref[:, hs] = o.astype(o_ref.dtype)


def _dsa_prompt(q, tail, kidx, kbf, vbf, tn, bsz, seq):
    nq = seq // CH
    wa = H_A * HEAD_DIM
    k_sel = min(DSA_TOPK, seq // 4)
    once = dict(pipeline_mode=pl.Buffered(1))
    return pl.pallas_call(
        functools.partial(_dsa_kernel, k_sel=k_sel),
        out_shape=jax.ShapeDtypeStruct(q.shape, BF16),
        grid=(bsz, nq),
        in_specs=[pl.BlockSpec((CH, wa), lambda b, i: (b * nq + i, 0)),
                  pl.BlockSpec((CH, tail.shape[1]), lambda b, i: (b * nq + i, 0)),
                  pl.BlockSpec((seq, D_IDX), lambda b, i: (b, 0), **once),
                  pl.BlockSpec((seq, wa), lambda b, i: (b, 0), **once),
                  pl.BlockSpec((seq, wa), lambda b, i: (b, 0), **once),
                  pl.BlockSpec((H_A, 2, LANE, LANE), lambda b, i: (0, 0, 0, 0), **once)],
        out_specs=pl.BlockSpec((CH, wa), lambda b, i: (b * nq + i, 0)),
        scratch_shapes=[pltpu.VMEM((CH, seq), F32), pltpu.VMEM((CH, seq), F32), pltpu.VMEM((3, CH, LANE), F32)],
        compiler_params=_cparams(2),
        name="dsa_prompt",
    )(q, tail, kidx, kbf, vbf, tn)


def _top_blocks(gate, n_valid, n_top):
    nblk = gate.shape[1]
    blk = lax.broadcasted_iota(I32, gate.shape, 1)
    g = jnp.where(blk < n_valid, gate, NEG_INF)
    rank = jnp.zeros(gate.shape, I32)
    for m in range(nblk):
        gm = g[:, m:m + 1]
        later = jnp.where(m < blk, 1, 0)
        rank = rank + jnp.where(gm > g, 1, 0) + jnp.where(gm == g, later, 0)
    return jnp.where((blk < n_valid) & (rank < n_top), 1.0, 0.0)


def _moba_kernel(o_in_ref, q_ref, km_ref, k_ref, v_ref, tn_ref, o_ref, ls_ref, st_ref, *, n_top):
    del o_in_ref
    i = pl.program_id(2)
    q = q_ref[...]
    gate = lax.dot_general(q, km_ref[...], (((1,), (1,)), ((), ())), preferred_element_type=F32,
                           precision=lax.Precision.HIGHEST)
    own = (i * CH + lax.broadcasted_iota(I32, (CH, 1), 0)) // MOBA_BLOCK
    blk = lax.broadcasted_iota(I32, gate.shape, 1)
    sel = jnp.where(blk == own, 1.0, _top_blocks(gate, own, n_top))
    q_bf = (q * SCALE).astype(BF16)
    per_ch = CH // MOBA_BLOCK

    def far(off):
        first = off // MOBA_BLOCK
        cols = []
        for t in range(per_ch):
            picked = jnp.sum(jnp.where(blk == first + t, sel, 0.0), axis=1, keepdims=True)
            cols.append(jnp.broadcast_to(jnp.where(picked > 0.5, 0.0, NEG_INF), (CH, MOBA_BLOCK)))
        return jnp.concatenate(cols, axis=1)

    o = _attend(q_bf, k_ref, v_ref, ls_ref, st_ref, i, far, tn_ref[0, 0], tn_ref[0, 1])
    o_ref[...] = o.astype(o_ref.dtype)


def _moba_prompt(o_ab, q, kmean, kbf, vbf, tn, bsz, seq):
    nq = seq // CH
    nblk = kmean.shape[1]
    n_top = min(MOBA_TOPK, nblk)
    return pl.pallas_call(
        functools.partial(_moba_kernel, n_top=n_top),
        out_shape=jax.ShapeDtypeStruct(o_ab.shape, o_ab.dtype),
        grid=(bsz, H_B, nq),
        in_specs=[pl.BlockSpec(memory_space=pl.ANY),
                  pl.BlockSpec((CH, HEAD_DIM), lambda b, h, i: (b * nq + i, H_A + h)),
                  pl.BlockSpec((None, nblk, HEAD_DIM), lambda b, h, i: (b, 0, H_A + h)),
                  pl.BlockSpec((seq, HEAD_DIM), lambda b, h, i: (b, H_A + h)),
                  pl.BlockSpec((seq, HEAD_DIM), lambda b, h, i: (b, H_A + h)),
                  pl.BlockSpec((1, 2, LANE, LANE), lambda b, h, i: (H_A + h, 0, 0, 0))],
        out_specs=pl.BlockSpec((CH, HEAD_DIM), lambda b, h, i: (b * nq + i, H_A + h)),
        scratch_shapes=[pltpu.VMEM((CH, seq), F32), pltpu.VMEM((3, CH, LANE), F32)],
        input_output_aliases={0: 0},
        compiler_params=_cparams(3),
        name="moba_prompt",
    )(o_ab, q, kmean, kbf, vbf, tn)


def _paged_specs(n, block, layer, pages_of, buffers=2):
    extra = {} if buffers == 2 else dict(pipeline_mode=pl.Buffered(buffers))
    def one(r):
        zeros = (0,) * (len(block) - 2)
        return pl.BlockSpec(block, lambda b, g, pt: (layer, pt[b, pages_of(b, g, r)]) + zeros, **extra)
    return [one(r) for r in range(n)]


def _idx_scores_kernel(pt_ref, *refs, n_pg):
    del pt_ref
    kp, (qi_ref, wi_ref, o_ref) = refs[:n_pg], refs[n_pg:]
    qi = qi_ref[...].astype(BF16)
    wi = wi_ref[...]
    for r in range(n_pg):
        s = lax.dot_general(qi, kp[r][...].astype(BF16), (((1,), (1,)), ((), ())), preferred_element_type=F32)
        o_ref[r:r + 1, :] = jnp.sum(jnp.maximum(s, 0.0) * wi, axis=0, keepdims=True)


def _idx_scores_sample(cache_kidx, li, page_table, qi8, wi8):
    nb, n_pages = page_table.shape
    page = cache_kidx.shape[2]
    n_pg = min(16, n_pages)
    blk = (None, None, page, D_IDX)
    return pl.pallas_call(
        functools.partial(_idx_scores_kernel, n_pg=n_pg),
        out_shape=jax.ShapeDtypeStruct((nb, n_pages, page), F32),
        grid_spec=pltpu.PrefetchScalarGridSpec(
            num_scalar_prefetch=1,
            grid=(nb, n_pages // n_pg),
            in_specs=_paged_specs(n_pg, blk, li, lambda b, g, r: g * n_pg + r)
            + [pl.BlockSpec((None, 8, D_IDX), lambda b, g, pt: (b, 0, 0)),
               pl.BlockSpec((None, 8, 1), lambda b, g, pt: (b, 0, 0))],
            out_specs=pl.BlockSpec((None, n_pg, page), lambda b, g, pt: (b, g, 0))),
        compiler_params=_cparams(2),
        name="idx_scores_sample",
    )(page_table, *([cache_kidx] * n_pg), qi8, wi8)


def _dsa_select_kernel(sc_ref, qi_ref, wi_ref, kin_ref, mb_ref, mbn_ref, *, k_sel):
    nb, n_pages, page = sc_ref.shape
    incl = jnp.where(lax.broadcasted_iota(I32, (page, page), 0) <= lax.broadcasted_iota(I32, (page, page), 1),
                     1.0, 0.0).astype(BF16)
    before = jnp.where(lax.broadcasted_iota(I32, (n_pages, n_pages), 1) < lax.broadcasted_iota(I32, (n_pages, n_pages), 0),
                       1.0, 0.0).astype(BF16)

    def total(x):
        return jnp.sum(jnp.sum(x, axis=0, keepdims=True), axis=1, keepdims=True)

    for b in range(nb):
        s_new = jnp.sum(qi_ref[b] * kin_ref[b:b + 1, :], axis=1, keepdims=True)
        s_new = jnp.sum(jnp.maximum(s_new, 0.0) * wi_ref[b], axis=0, keepdims=True)
        key = _ordered_key(sc_ref[b])
        key_new = _ordered_key(jnp.broadcast_to(s_new, (8, LANE)))[0:1, 0:1]

        def count_ge(t):
            return total(jnp.where(key >= t, 1, 0)) + jnp.where(key_new >= t, 1, 0)

        thr = _kth_largest_key(count_ge, k_sel, (1, 1))
        gt = key > thr
        need = (k_sel - total(jnp.where(gt, 1, 0)) - jnp.where(key_new > thr, 1, 0)).astype(F32)
        eq = key == thr
        eq_bf = jnp.where(eq, 1.0, 0.0).astype(BF16)
        pre = jnp.dot(eq_bf, incl, preferred_element_type=F32)
        off = jnp.sum(jnp.dot(before, eq_bf, preferred_element_type=F32), axis=1, keepdims=True)
        sel = gt | (eq & (pre + off <= need))
        mb_ref[b] = jnp.where(sel, 0.0, NEG_INF)
        n_eq = total(jnp.where(eq, 1.0, 0.0))
        sel_new = (key_new > thr) | ((key_new == thr) & (n_eq + 1.0 <= need))
        mbn_ref[b:b + 1, :] = jnp.broadcast_to(jnp.where(sel_new, 0.0, NEG_INF), (1, LANE))


def _dsa_select_sample(scores, qi8, wi8, ki_new):
    nb, n_pages, page = scores.shape
    k_sel = min(DSA_TOPK, (n_pages * page + 1) // 4)
    return pl.pallas_call(
        functools.partial(_dsa_select_kernel, k_sel=k_sel),
        out_shape=[jax.ShapeDtypeStruct(scores.shape, F32), jax.ShapeDtypeStruct((nb, LANE), F32)],
        compiler_params=pltpu.CompilerParams(vmem_limit_bytes=VMEM_LIMIT),
        name="dsa_select_sample",
    )(scores, qi8, wi8, ki_new)


def _logf_suffix_sample_kernel(pt_ref, *refs, n_pg):
    del pt_ref
    lp, (fg_ref, bf_ref, r_ref, lfn_ref, carry) = refs[:n_pg], refs[n_pg:]
    page = lp[0].shape[0]

    @pl.when(pl.program_id(1) == 0)
    def _():
        z = fg_ref[...] + bf_ref[...]
        lfn = jnp.minimum(z, 0.0) - jnp.log(1.0 + jnp.exp(-jnp.abs(z)))
        lfn_ref[...] = lfn
        carry[...] = lfn

    row = lax.broadcasted_iota(I32, (page, H_C), 0)
    c = carry[...]
    for r in reversed(range(n_pg)):
        lf = lp[r][...]
        x = lf
        k = 1
        while k < page:
            x = x + jnp.where(row + k < page, pltpu.roll(x, page - k, axis=0), 0.0)
            k *= 2
        r_ref[r * page:(r + 1) * page, :] = x - lf + c
        c = c + x[0:1, :]
    carry[...] = c


def _logf_suffix_sample(cache_logf, li, page_table, fg_new, b_f):
    nb, n_pages = page_table.shape
    page = cache_logf.shape[2]
    n_pg = min(16, n_pages)
    n_steps = n_pages // n_pg
    blk = (None, None, page, H_C)
    return pl.pallas_call(
        functools.partial(_logf_suffix_sample_kernel, n_pg=n_pg),
        out_shape=[jax.ShapeDtypeStruct((nb, n_pages * page, H_C), F32), jax.ShapeDtypeStruct((nb, 1, H_C), F32)],
        grid_spec=pltpu.PrefetchScalarGridSpec(
            num_scalar_prefetch=1,
            grid=(nb, n_steps),
            in_specs=_paged_specs(n_pg, blk, li, lambda b, g, r: (n_steps - 1 - g) * n_pg + r)
            + [pl.BlockSpec((None, 1, H_C), lambda b, g, pt: (b, 0, 0)),
               pl.BlockSpec((1, H_C), lambda b, g, pt: (0, 0))],
            out_specs=[pl.BlockSpec((None, n_pg * page, H_C), lambda b, g, pt: (b, n_steps - 1 - g, 0)),
                       pl.BlockSpec((None, 1, H_C), lambda b, g, pt: (b, 0, 0))],
            scratch_shapes=[pltpu.VMEM((1, H_C), F32)]),
        compiler_params=_cparams(2),
        name="logf_suffix_sample",
    )(page_table, *([cache_logf] * n_pg), fg_new, b_f)


def _decode_partials_kernel(pt_ref, *refs, n_pg, has_mask):
    del pt_ref
    kp, vp, rest = refs[:n_pg], refs[n_pg:2 * n_pg], refs[2 * n_pg:]
    if has_mask:
        qbd_ref, ebd_ref, bias_ref, mask_ref, m_ref, l_ref, acc_ref, ks_ref, kbf = rest
    else:
        qbd_ref, ebd_ref, bias_ref, m_ref, l_ref, acc_ref, ks_ref, kbf = rest
    page = kp[0].shape[0] // H_AB
    heads = [slice(h * HEAD_DIM, (h + 1) * HEAD_DIM) for h in range(H_AB)]
    for r in range(n_pg):
        kt = pltpu.einshape("phd->hpd", kp[r][...].reshape(page, H_AB, HEAD_DIM))
        for h in range(H_AB):
            kbf[r * page:(r + 1) * page, heads[h]] = kt[h].astype(BF16)
    logits = jnp.dot(kbf[...], qbd_ref[...], preferred_element_type=F32)
    s = logits[:, :H_AB] + bias_ref[...]
    if has_mask:
        eye = lax.broadcasted_iota(I32, (page, page), 0) == lax.broadcasted_iota(I32, (page, page), 1)
        first = pl.program_id(1) * n_pg
        cols = [jnp.sum(jnp.where(eye, mask_ref[pl.ds(first + r, 1), :], 0.0), axis=1, keepdims=True)
                for r in range(n_pg)]
        lane = lax.broadcasted_iota(I32, s.shape, 1)
        s = s + jnp.where(lane < H_A, jnp.concatenate(cols, axis=0), 0.0)
    per_blk = MOBA_BLOCK // page
    for blk in range(n_pg // per_blk):
        rows = slice(blk * MOBA_BLOCK, (blk + 1) * MOBA_BLOCK)
        sb = s[rows]
        m = jnp.max(sb, axis=0, keepdims=True)
        p = jnp.exp(sb - m)
        m_ref[blk:blk + 1, :] = m
        l_ref[blk:blk + 1, :] = jnp.sum(p, axis=0, keepdims=True)
        pexp = jnp.dot(p.astype(BF16), ebd_ref[...], preferred_element_type=F32)
        pages = range(blk * per_blk, (blk + 1) * per_blk)
        vt = [pltpu.einshape("phd->hpd", vp[pg][...].reshape(page, H_AB, HEAD_DIM)) for pg in pages]
        for h in range(H_AB):
            acc_ref[blk:blk + 1, heads[h]] = sum(
                jnp.sum(pexp[n * page:(n + 1) * page, heads[h]] * vt[n][h], axis=0, keepdims=True)
                for n in range(per_blk))
        ks_ref[blk] = sum(jnp.sum(kp[pg][...].reshape(page, H_AB, HEAD_DIM), axis=0) for pg in pages)


def _decode_partials(cache_k, cache_v, li, page_table, qbd, ebd, bias, mask):
    nb, n_pages = page_table.shape
    page = cache_k.shape[2] // H_AB
    n_pg = min(4 * (MOBA_BLOCK // page), n_pages)
    n_steps = n_pages // n_pg
    nblk = n_pg * page // MOBA_BLOCK
    rows = n_pg * page
    blk = (None, None, page * H_AB, HEAD_DIM)
    pages_of = lambda b, g, r: g * n_pg + r
    per_b = bias.shape[0] > 1
    in_specs = (_paged_specs(n_pg, blk, li, pages_of) + _paged_specs(n_pg, blk, li, pages_of)
                + [pl.BlockSpec((None, D_HEADS, LANE), lambda b, g, pt: (b, 0, 0)),
                   pl.BlockSpec((H_AB, D_HEADS), lambda b, g, pt: (0, 0)),
                   pl.BlockSpec((None, rows, H_AB), lambda b, g, pt: (b if per_b else 0, g, 0))])
    args = [page_table] + [cache_k] * n_pg + [cache_v] * n_pg + [qbd, ebd, bias]
    if mask is not None:
        in_specs.append(pl.BlockSpec((None, n_pages, page), lambda b, g, pt: (b, 0, 0)))
        args.append(mask)
    small = pl.BlockSpec((None, None, nblk, H_AB), lambda b, g, pt: (b, g, 0, 0))
    wide = pl.BlockSpec((None, None, nblk, D_HEADS), lambda b, g, pt: (b, g, 0, 0))
    wide3 = pl.BlockSpec((None, None, nblk, H_AB, HEAD_DIM), lambda b, g, pt: (b, g, 0, 0, 0))
    m, l, acc, ks = pl.pallas_call(
        functools.partial(_decode_partials_kernel, n_pg=n_pg, has_mask=mask is not None),
        out_shape=[jax.ShapeDtypeStruct((nb, n_steps, nblk, H_AB), F32)] * 2
        + [jax.ShapeDtypeStruct((nb, n_steps, nblk, D_HEADS), F32),
           jax.ShapeDtypeStruct((nb, n_steps, nblk, H_AB, HEAD_DIM), F32)],
        grid_spec=pltpu.PrefetchScalarGridSpec(
            num_scalar_prefetch=1,
            grid=(nb, n_steps),
            in_specs=in_specs,
            out_specs=[small, small, wide, wide3],
            scratch_shapes=[pltpu.VMEM((rows, D_HEADS), BF16)]),
        compiler_params=_cparams(2),
        name="decode_partials",
    )(*args)
    nb_all = n_steps * nblk
    return (m.reshape(nb, nb_all, H_AB), l.reshape(nb, nb_all, H_AB),
            acc.reshape(nb, nb_all, D_HEADS), ks.reshape(nb, nb_all, H_AB, HEAD_DIM))


def _decode_combine_kernel(m_ref, l_ref, acc_ref, ks_ref, q_ref, kn_ref, vn_ref, bn_ref, o_ref, *, moba_from, n_top):
    q = q_ref[...]
    prod = q * kn_ref[...]
    nblk = m_ref.shape[0]
    blk = lax.broadcasted_iota(I32, (nblk, 1), 0)
    for h in range(H_AB):
        hs = slice(h * HEAD_DIM, (h + 1) * HEAD_DIM)
        lg_new = jnp.sum(prod[:, hs], axis=1, keepdims=True) * SCALE + bn_ref[:, h:h + 1]
        m_h = m_ref[:, h:h + 1]
        if h >= moba_from:
            gate = jnp.sum(ks_ref[:, h, :] * (1.0 / MOBA_BLOCK) * q[:, hs], axis=1, keepdims=True)
            keep = jnp.zeros((nblk, 1), jnp.bool_)
            for _ in range(n_top):
                best = jnp.max(gate, axis=0, keepdims=True)
                first = jnp.min(jnp.where(gate == best, blk, nblk), axis=0, keepdims=True)
                keep = keep | (blk == first)
                gate = jnp.where(blk == first, -jnp.inf, gate)
            m_h = jnp.where(keep, m_h, -jnp.inf)
        top = jnp.maximum(jnp.max(m_h, axis=0, keepdims=True), lg_new)
        w = jnp.exp(m_h - top)
        e_new = jnp.exp(lg_new - top)
        den = jnp.sum(w * l_ref[:, h:h + 1], axis=0, keepdims=True) + e_new
        num = jnp.sum(w * acc_ref[:, hs], axis=0, keepdims=True) + e_new * vn_ref[:, hs]
        o_ref[:, hs] = (num / den).astype(o_ref.dtype)


def _decode_combine(m, l, acc, ks, q, k_new, v_new, bias_new, moba_from):
    nb, nblk, _ = m.shape
    n_top = min(MOBA_TOPK, nblk)
    small = pl.BlockSpec((None, nblk, H_AB), lambda b: (b, 0, 0))
    wide = pl.BlockSpec((None, nblk, D_HEADS), lambda b: (b, 0, 0))
    row = pl.BlockSpec((None, 1, D_HEADS), lambda b: (b, 0, 0))
    out = pl.pallas_call(
        functools.partial(_decode_combine_kernel, moba_from=moba_from, n_top=n_top),
        out_shape=jax.ShapeDtypeStruct((nb, 1, D_HEADS), BF16),
        grid=(nb,),
        in_specs=[small, small, wide, pl.BlockSpec((None, nblk, H_AB, HEAD_DIM), lambda b: (b, 0, 0, 0)),
                  row, row, row, pl.BlockSpec((None, 1, LANE), lambda b: (b, 0, 0))],
        out_specs=row,
        compiler_params=_cparams(1),
        name="decode_combine",
    )(m, l, acc, ks, q.reshape(nb, 1, D_HEADS), k_new.reshape(nb, 1, D_HEADS), v_new.reshape(nb, 1, D_HEADS),
      bias_new.reshape(nb, 1, LANE))
    return out.reshape(nb, D_HEADS)


def _ab_cols(tn):
    per = H_A * HEAD_DIM // tn
    def cols(j):
        base = (j // per) * 3 * per + j % per
        return base, base + per, base + 2 * per
    return cols


def _c_cols(tn):
    per = D_HEADS // tn
    return lambda j: (j, j + per, j + 2 * per)


def _block_diag_queries(q):
    head_of_row = jnp.arange(D_HEADS, dtype=I32) // HEAD_DIM
    onehot = (head_of_row[:, None] == jnp.arange(LANE, dtype=I32)[None, :]).astype(F32)
    return ((q * SCALE)[:, :, None] * onehot[None]).astype(BF16)


def kernel(x_prompt, x_sample, cache_k_ab, cache_v_ab, cache_kidx, cache_k_c, cache_v_c, cache_logf_c, state_conv,
           page_table, g_mix, g_ffn, g_final, w_in_ab, w_out_ab, t5_table, w_in_c, b_forget, w_out_c, w_up, conv_w,
           conv_b, w_down):
    bsz, seq, d = x_prompt.shape
    nb, dseq, _ = x_sample.shape
    depth = g_mix.shape[0]
    d_ff = w_down.shape[1]
    n_pages = page_table.shape[1]
    page = cache_k_ab.shape[2]
    past = n_pages * page
    assert dseq == 1 and d == D_HEADS and seq % CH == 0 and past % MOBA_BLOCK == 0
    t_p = bsz * seq
    tn_ffn = 512 if d_ff % 512 == 0 else LANE

    qkv_w = 3 * D_HEADS
    n_tail_ab = w_in_ab.shape[-1] - qkv_w
    n_tail_c = w_in_c.shape[-1] - qkv_w
    w_tail_ab = jnp.pad(w_in_ab[:, :, qkv_w:], ((0, 0), (0, 0), (0, TAIL_PAD - n_tail_ab)))
    w_tail_c = jnp.pad(w_in_c[:, :, qkv_w:], ((0, 0), (0, 0), (0, LANE - n_tail_c)))
    conv_b3 = conv_b.reshape(depth, 1, 2 * d_ff)
    ck_ab, cv_ab, ck_c, cv_c = (c.reshape(c.shape[0], c.shape[1], page * H_AB, HEAD_DIM)
                                for c in (cache_k_ab, cache_v_ab, cache_k_c, cache_v_c))

    tn = _t5_near_tiles(t5_table)
    t5_past = _t5_lookup(t5_table, past - jnp.arange(past, dtype=I32))[None]
    t5_self = jnp.pad(_t5_lookup(t5_table, jnp.zeros((1,), I32)), ((0, 0), (0, LANE - H_AB)))
    lane = jnp.arange(LANE, dtype=I32)[None]
    ebd = (jnp.arange(H_AB, dtype=I32)[:, None] == (jnp.arange(D_HEADS, dtype=I32) // HEAD_DIM)[None, :]).astype(BF16)
    zero_conv = jnp.zeros((bsz, CONV_W - 1, 2 * d_ff), F32)
    iq, ik, iw = H_IDX * D_IDX, H_IDX * D_IDX + D_IDX, H_IDX * D_IDX + D_IDX + H_IDX

    xp = x_prompt.reshape(t_p, d)
    xs = x_sample.reshape(nb, d)
    outs = {name: [] for name in ("kidx_p", "lf_p", "cv_p", "kidx_s", "lf_s", "cv_s")}
    n_ab, n_c = (depth + 1) // 2, depth // 2
    kv_ab_p = kv_ab_s = kv_c_p = kv_c_s = None
    for layer in range(depth):
        li = layer // 2
        hp = _rmsnorm(xp, g_mix[layer], BF16)
        hs = _rmsnorm(xs, g_mix[layer], BF16)
        if layer % 2 == 0:
            q, k_st, v_st, kbf, vbf, kmean = _qkv_proj(hp, w_in_ab, li, _ab_cols(256), True, n_ab, kv_ab_p)
            kv_ab_p = (k_st, v_st)
            tail = _matmul(hp, w_tail_ab, li, TAIL_PAD)
            kidx = tail[:, iq:ik]
            o = _dsa_prompt(q, tail, kidx, kbf, vbf, tn, bsz, seq)
            o = _moba_prompt(o, q, kmean.reshape(bsz, seq // MOBA_BLOCK, D_HEADS), kbf, vbf, tn, bsz, seq)
            xp = _matmul(o, w_out_ab, li, 512, res=xp)
            outs["kidx_p"].append(kidx.reshape(bsz, seq, D_IDX))
            q, k_st, v_st, _, _ = _qkv_proj(hs, w_in_ab, li, _ab_cols(256), False, n_ab, kv_ab_s)
            kv_ab_s = (k_st, v_st)
            k, v = k_st[li], v_st[li]
            tail = _matmul(hs, w_tail_ab, li, TAIL_PAD)
            qi8 = jnp.pad(tail[:, :iq].reshape(nb, H_IDX, D_IDX), ((0, 0), (0, 8 - H_IDX), (0, 0)))
            wi8 = jnp.pad(tail[:, ik:iw].reshape(nb, H_IDX, 1), ((0, 0), (0, 8 - H_IDX), (0, 0)))
            ki_new = tail[:, iq:ik]
            scores = _idx_scores_sample(cache_kidx, li, page_table, qi8, wi8)
            mb, mb_new = _dsa_select_sample(scores, qi8, wi8, ki_new)
            parts = _decode_partials(ck_ab, cv_ab, li, page_table, _block_diag_queries(q), ebd, t5_past,
                                     mb)
            bias_new = t5_self + jnp.where(lane < H_A, mb_new, 0.0)
            o = _decode_combine(*parts, q, k, v, bias_new, H_A)
            xs = _matmul(o, w_out_ab, li, 512, res=xs)
            outs["kidx_s"].append(ki_new.reshape(nb, 1, D_IDX))
        else:
            q, k_st, v_st, kbf, vbf = _qkv_proj(hp, w_in_c, li, _c_cols(256), False, n_c, kv_c_p)
            kv_c_p = (k_st, v_st)
            tail = _matmul(hp, w_tail_c, li, LANE)
            fg_t = jnp.transpose(tail[:, :H_C].reshape(bsz, seq, H_C), (0, 2, 1)).reshape(bsz * H_C, seq)
            lf_t, r_t = _logf_suffix(fg_t, jnp.tile(b_forget[li], bsz).reshape(bsz * H_C, 1))
            o = _fox_prompt(q, kbf, vbf, r_t.reshape(bsz, H_C, 1, seq), r_t.reshape(bsz, H_C, seq, 1), bsz, seq)
            xp = _matmul(o, w_out_c, li, 512, res=xp)
            outs["lf_p"].append(jnp.transpose(lf_t.reshape(bsz, H_C, seq), (0, 2, 1)))
            q, k_st, v_st, _, _ = _qkv_proj(hs, w_in_c, li, _c_cols(256), False, n_c, kv_c_s)
            kv_c_s = (k_st, v_st)
            k, v = k_st[li], v_st[li]
            tail = _matmul(hs, w_tail_c, li, LANE)
            r_past, lf_new = _logf_suffix_sample(cache_logf_c, li, page_table, tail[:, :H_C].reshape(nb, 1, H_C),
                                                 b_forget[li].reshape(1, H_C))
            parts = _decode_partials(ck_c, cv_c, li, page_table, _block_diag_queries(q), ebd, r_past, None)
            o = _decode_combine(*parts, q, k, v, jnp.zeros((nb, LANE), F32), H_C)
            xs = _matmul(o, w_out_c, li, 512, res=xs)
            outs["lf_s"].append(lf_new)
        act, sg, sv = _ffn_up(_rmsnorm(xp, g_ffn[layer], BF16), w_up, conv_w, conv_b3, zero_conv, layer, seq, tn=tn_ffn)
        xp = _matmul(act, w_down, layer, 512, res=xp)
        outs["cv_p"].append(jnp.concatenate([sg, sv], axis=-1))
        act, sg, sv = _ffn_up(_rmsnorm(xs, g_ffn[layer], BF16), w_up, conv_w, conv_b3, state_conv[layer], layer, 1,
                              tn=tn_ffn)
        xs = _matmul(act, w_down, layer, 512, res=xs)
        outs["cv_s"].append(jnp.concatenate([sg, sv], axis=-1))
    y_prompt = _rmsnorm(xp, g_final, F32).reshape(bsz, seq, d)
    y_sample = _rmsnorm(xs, g_final, F32).reshape(nb, 1, d)
    st = lambda name: jnp.stack(outs[name])
    heads_p = lambda a: a.reshape(a.shape[0], bsz, seq, H_AB, HEAD_DIM)
    heads_s = lambda a: a.reshape(a.shape[0], nb, 1, H_AB, HEAD_DIM)
    return (y_prompt, y_sample, heads_p(kv_ab_p[0]), heads_p(kv_ab_p[1]), st("kidx_p"), heads_p(kv_c_p[0]),
            heads_p(kv_c_p[1]), st("lf_p"), st("cv_p"), heads_s(kv_ab_s[0]), heads_s(kv_ab_s[1]), st("kidx_s"),
            heads_s(kv_c_s[0]), heads_s(kv_c_s[1]), st("lf_s"), st("cv_s"))
```
